```python
import jax, jax.numpy as jnp
from jax import lax
import numpy as np

D_MODEL = 2048
BATCH = 4
SEQ = 2048
DEPTH = 2

MEM_LEN = 256
D_MIX = D_MODEL
EPS = 1e-6
MASK_VALUE = -1e30
FORCE_SCORE = 1e4
NSA_HEADS = 16
NSA_KV_GROUPS = 2
HEADS_PER_GROUP = NSA_HEADS // NSA_KV_GROUPS
HEAD_DIM = 64
NSA_WIDTH = NSA_HEADS * HEAD_DIM
KV_WIDTH = NSA_KV_GROUPS * HEAD_DIM
N_BRANCH = 3
CMP_BLOCK = 32
CMP_STRIDE = 16
CMP_HIDDEN = 4 * HEAD_DIM
SEL_BLOCK = 64
N_SEL = 8
WINDOW = 512
Q_BLOCK = 128
CONV_WIDTH = D_MIX - NSA_WIDTH
CONV_K = 3
IN_SIZES = (NSA_WIDTH, KV_WIDTH, KV_WIDTH, KV_WIDTH, KV_WIDTH, KV_WIDTH, KV_WIDTH,
            N_BRANCH * NSA_HEADS, CONV_WIDTH, CONV_WIDTH, CONV_WIDTH)
IN_COLS = sum(IN_SIZES)
XA_HEADS = 4
XA_HEAD_DIM = 128
XA_WIDTH = XA_HEADS * XA_HEAD_DIM
D_FF = 5632

kernel_name = "nsa_shortconv_macaron_hybrid"


def rms_norm(x, g):
    xf = x.astype(jnp.float32)
    y = xf * lax.rsqrt(jnp.mean(xf * xf, axis=-1, keepdims=True) + EPS)
    return (y * g.astype(jnp.float32)).astype(x.dtype)


def masked_softmax(s, mask):
    s = jnp.where(mask, s.astype(jnp.float32), MASK_VALUE)
    e = jnp.where(mask, jnp.exp(s - jnp.max(s, axis=-1, keepdims=True)), 0.0)
    denom = jnp.sum(e, axis=-1, keepdims=True)
    return e / jnp.where(denom > 0.0, denom, 1.0)


def swiglu_ffn(x, g, w_gate, w_up, w_down):
    h = rms_norm(x, g)
    return (jax.nn.silu(h @ w_gate) * (h @ w_up)) @ w_down


def split_columns(proj):
    offsets = [int(v) for v in np.cumsum(IN_SIZES)[:-1]]
    return jnp.split(proj, offsets, axis=-1)


def compress_blocks(kv, pe, w1, w2):
    B, S, G, Dh = kv.shape
    n_cmp = (S - CMP_BLOCK) // CMP_STRIDE + 1
    idx = np.arange(n_cmp)[:, None] * CMP_STRIDE + np.arange(CMP_BLOCK)[None, :]
    blocks = kv[:, idx] + pe[None, None, :, None, :]
    blocks = blocks.transpose(0, 1, 3, 2, 4).reshape(B, n_cmp, G, CMP_BLOCK * Dh)
    return jax.nn.gelu(blocks @ w1) @ w2


def nsa_attention(q, kc, vc, ks, vs, kw, vw, gate_logits,
                  cmp_pe_k, cmp_w1_k, cmp_w2_k, cmp_pe_v, cmp_w1_v, cmp_w2_v,
                  q_norm, k_norm):
    B, S, _ = q.shape
    G, Z, Dh = NSA_KV_GROUPS, HEADS_PER_GROUP, HEAD_DIM
    scale = Dh ** -0.5
    q = rms_norm(q.reshape(B, S, G, Z, Dh), q_norm)
    kc, vc, ks, vs, kw, vw = (a.reshape(B, S, G, Dh) for a in (kc, vc, ks, vs, kw, vw))
    t = np.arange(S)

    n_cmp = (S - CMP_BLOCK) // CMP_STRIDE + 1
    k_cmp = rms_norm(compress_blocks(kc, cmp_pe_k, cmp_w1_k, cmp_w2_k), k_norm[0])
    v_cmp = compress_blocks(vc, cmp_pe_v, cmp_w1_v, cmp_w2_v)
    cmp_end = np.arange(n_cmp) * CMP_STRIDE + CMP_BLOCK - 1
    s_c = jnp.einsum('bsgzd,bngd->bgzsn', q, k_cmp) * scale
    p_c = masked_softmax(s_c, cmp_end[None, :] <= t[:, None])
    o_c = jnp.einsum('bgzsn,bngd->bsgzd', p_c.astype(v_cmp.dtype), v_cmp)

    n_blk = S // SEL_BLOCK
    n_sel = min(N_SEL, n_blk)
    cmp_start = np.arange(n_cmp) * CMP_STRIDE
    sel_start = np.arange(n_blk) * SEL_BLOCK
    overlap = ((cmp_start[:, None] < sel_start[None, :] + SEL_BLOCK)
               & (cmp_start[:, None] + CMP_BLOCK > sel_start[None, :])).astype(np.float32)
    importance = jnp.einsum('bgzsn,nj->bgsj', p_c, jnp.asarray(overlap))
    blk = np.arange(n_blk)[None, :]
    cur = (t // SEL_BLOCK)[:, None]
    forced = (blk == 0) | (blk == cur) | (blk == cur - 1)
    future = blk > cur
    score = jnp.where(future, -1.0, jnp.where(forced, FORCE_SCORE, importance))
    _, sel_idx = lax.top_k(score, n_sel)

    ks_blk = rms_norm(ks, k_norm[1]).reshape(B, n_blk, SEL_BLOCK, G, Dh).transpose(0, 3, 1, 2, 4)
    vs_blk = vs.reshape(B, n_blk, SEL_BLOCK, G, Dh).transpose(0, 3, 1, 2, 4)
    pad = ((0, 0), (WINDOW, 0), (0, 0), (0, 0))
    kw_pad = jnp.pad(rms_norm(kw, k_norm[2]), pad)
    vw_pad = jnp.pad(vw, pad)
    b_ix = jnp.arange(B)[:, None, None, None]
    g_ix = jnp.arange(G)[None, :, None, None]
    n_keys_sel = n_sel * SEL_BLOCK

    def query_block(args):
        qb, ib, start = args
        tq = start + jnp.arange(Q_BLOCK)
        k_sel = ks_blk[b_ix, g_ix, ib].reshape(B, G, Q_BLOCK, n_keys_sel, Dh)
        v_sel = vs_blk[b_ix, g_ix, ib].reshape(B, G, Q_BLOCK, n_keys_sel, Dh)
        pos = (ib[..., None] * SEL_BLOCK + jnp.arange(SEL_BLOCK)).reshape(B, G, 1, Q_BLOCK, n_keys_sel)
        s_s = jnp.einsum('bqgzd,bgqkd->bgzqk', qb, k_sel) * scale
        p_s = masked_softmax(s_s, pos <= tq[:, None])
        o_s = jnp.einsum('bgzqk,bgqkd->bqgzd', p_s.astype(v_sel.dtype), v_sel)
        k_win = lax.dynamic_slice_in_dim(kw_pad, start, Q_BLOCK + WINDOW, axis=1)
        v_win = lax.dynamic_slice_in_dim(vw_pad, start, Q_BLOCK + WINDOW, axis=1)
        spos = start - WINDOW + jnp.arange(Q_BLOCK + WINDOW)
        lag = tq[:, None] - spos[None, :]
        m_w = (spos[None, :] >= 0) & (lag >= 0) & (lag < WINDOW)
        s_w = jnp.einsum('bqgzd,bkgd->bgzqk', qb, k_win) * scale
        p_w = masked_softmax(s_w, m_w)
        o_w = jnp.einsum('bgzqk,bkgd->bqgzd', p_w.astype(v_win.dtype), v_win)
        return o_s, o_w

    n_q = S // Q_BLOCK
    q_blocks = q.reshape(B, n_q, Q_BLOCK, G, Z, Dh).swapaxes(0, 1)
    idx_blocks = sel_idx.reshape(B, G, n_q, Q_BLOCK, n_sel).transpose(2, 0, 1, 3, 4)
    starts = jnp.arange(n_q, dtype=jnp.int32) * Q_BLOCK
    o_s, o_w = lax.map(query_block, (q_blocks, idx_blocks, starts))
    o_s = o_s.swapaxes(0, 1).reshape(B, S, G, Z, Dh)
    o_w = o_w.swapaxes(0, 1).reshape(B, S, G, Z, Dh)

    gates = jax.nn.sigmoid(gate_logits.reshape(B, S, N_BRANCH, G, Z, 1))
    o = gates[:, :, 0] * o_c + gates[:, :, 1] * o_s + gates[:, :, 2] * o_w
    return o.reshape(B, S, NSA_WIDTH)


def causal_dwconv(u, w):
    return lax.conv_general_dilated(u, w[:, None, :], window_strides=(1,),
                                    padding=((CONV_K - 1, 0),),
                                    dimension_numbers=('NWC', 'WIO', 'NWC'),
                                    feature_group_count=u.shape[-1])


def hybrid_mixer(x, mix_norm, w_in, cmp_pe_k, cmp_w1_k, cmp_w2_k, cmp_pe_v, cmp_w1_v,
                 cmp_w2_v, q_norm, k_norm, conv_w, out_norm_nsa, out_norm_conv, w_out):
    h = rms_norm(x, mix_norm)
    (q, kc, vc, ks, vs, kw, vw, gate_logits,
     b_gate, c_gate, xv) = split_columns(h @ w_in)
    o_a = nsa_attention(q, kc, vc, ks, vs, kw, vw, gate_logits,
                        cmp_pe_k, cmp_w1_k, cmp_w2_k, cmp_pe_v, cmp_w1_v, cmp_w2_v,
                        q_norm, k_norm)
    o_b = b_gate * causal_dwconv(c_gate * xv, conv_w)
    merged = jnp.concatenate([rms_norm(o_a, out_norm_nsa), rms_norm(o_b, out_norm_conv)], axis=-1)
    return merged @ w_out


def memory_cross_attention(x, mem, xattn_norm, mem_norm, w_q, w_kv, q_norm, k_norm, w_o):
    B, S, _ = x.shape
    M = mem.shape[1]
    h = rms_norm(x, xattn_norm)
    m = rms_norm(mem, mem_norm)
    q = rms_norm((h @ w_q).reshape(B, S, XA_HEADS, XA_HEAD_DIM), q_norm)
    k, v = jnp.split(m @ w_kv, 2, axis=-1)
    k = rms_norm(k.reshape(B, M, XA_HEADS, XA_HEAD_DIM), k_norm)
    v = v.reshape(B, M, XA_HEADS, XA_HEAD_DIM)
    s = jnp.einsum('bshd,bmhd->bhsm', q, k) * XA_HEAD_DIM ** -0.5
    p = jax.nn.softmax(s.astype(jnp.float32), axis=-1).astype(v.dtype)
    o = jnp.einsum('bhsm,bmhd->bshd', p, v).reshape(B, S, XA_WIDTH)
    return o @ w_o


def setup_inputs(seed: int = 0) -> dict:
    key = jax.random.key(seed)
    keys = iter(jax.random.split(key, 64))
    f32 = jnp.float32
    L = DEPTH

    def dense(shape, fan_in):
        return jax.random.normal(next(keys), shape, f32) * fan_in ** -0.5

    def gain(shape):
        return 1.0 + 0.02 * jax.random.normal(next(keys), shape, f32)

    def small(shape, s):
        return s * jax.random.normal(next(keys), shape, f32)

    return {
        "x": jax.random.normal(next(keys), (BATCH, SEQ, D_MODEL), f32),
        "mem": jax.random.normal(next(keys), (BATCH, MEM_LEN, D_MODEL), f32),
        "ffn1_norm": gain((L, D_MODEL)),
        "ffn1_w_gate": dense((L, D_MODEL, D_FF), D_MODEL),
        "ffn1_w_up": dense((L, D_MODEL, D_FF), D_MODEL),
        "ffn1_w_down": dense((L, D_FF, D_MODEL), D_FF),
        "mix_norm": gain((L, D_MODEL)),
        "w_in": dense((L, D_MODEL, IN_COLS), D_MODEL),
        "cmp_pe_k": small((L, CMP_BLOCK, HEAD_DIM), 0.1),
        "cmp_w1_k": dense((L, CMP_BLOCK * HEAD_DIM, CMP_HIDDEN), CMP_BLOCK * HEAD_DIM),
        "cmp_w2_k": dense((L, CMP_HIDDEN, HEAD_DIM), CMP_HIDDEN),
        "cmp_pe_v": small((L, CMP_BLOCK, HEAD_DIM), 0.1),
        "cmp_w1_v": dense((L, CMP_BLOCK * HEAD_DIM, CMP_HIDDEN), CMP_BLOCK * HEAD_DIM),
        "cmp_w2_v": dense((L, CMP_HIDDEN, HEAD_DIM), CMP_HIDDEN),
        "q_norm": gain((L, HEAD_DIM)),
        "k_norm": gain((L, N_BRANCH, HEAD_DIM)),
        "conv_w": dense((L, CONV_K, CONV_WIDTH), CONV_K),
        "out_norm_nsa": gain((L, NSA_WIDTH)),
        "out_norm_conv": gain((L, CONV_WIDTH)),
        "w_out": dense((L, D_MIX, D_MODEL), D_MIX),
        "xattn_norm": gain((L, D_MODEL)),
        "mem_norm": gain((L, D_MODEL)),
        "xattn_w_q": dense((L, D_MODEL, XA_WIDTH), D_MODEL),
        "xattn_w_kv": dense((L, D_MODEL, 2 * XA_WIDTH), D_MODEL),
        "xattn_q_norm": gain((L, XA_HEAD_DIM)),
        "xattn_k_norm": gain((L, XA_HEAD_DIM)),
        "xattn_w_o": dense((L, XA_WIDTH, D_MODEL), XA_WIDTH),
        "ffn2_norm": gain((L, D_MODEL)),
        "ffn2_w_gate": dense((L, D_MODEL, D_FF), D_MODEL),
        "ffn2_w_up": dense((L, D_MODEL, D_FF), D_MODEL),
        "ffn2_w_down": dense((L, D_FF, D_MODEL), D_FF),
    }


def reference(x, mem, ffn1_norm, ffn1_w_gate, ffn1_w_up, ffn1_w_down, mix_norm, w_in,
              cmp_pe_k, cmp_w1_k, cmp_w2_k, cmp_pe_v, cmp_w1_v, cmp_w2_v, q_norm, k_norm,
              conv_w, out_norm_nsa, out_norm_conv, w_out, xattn_norm, mem_norm,
              xattn_w_q, xattn_w_kv, xattn_q_norm, xattn_k_norm, xattn_w_o,
              ffn2_norm, ffn2_w_gate, ffn2_w_up, ffn2_w_down):
    for l in range(DEPTH):
        x = x + 0.5 * swiglu_ffn(x, ffn1_norm[l], ffn1_w_gate[l], ffn1_w_up[l], ffn1_w_down[l])
        x = x + hybrid_mixer(x, mix_norm[l], w_in[l], cmp_pe_k[l], cmp_w1_k[l], cmp_w2_k[l],
                             cmp_pe_v[l], cmp_w1_v[l], cmp_w2_v[l], q_norm[l], k_norm[l],
                             conv_w[l], out_norm_nsa[l], out_norm_conv[l], w_out[l])
        x = x + memory_cross_attention(x, mem, xattn_norm[l], mem_norm[l], xattn_w_q[l],
                                       xattn_w_kv[l], xattn_q_norm[l], xattn_k_norm[l],
                                       xattn_w_o[l])
        x = x + 0.5 * swiglu_ffn(x, ffn2_norm[l], ffn2_w_gate[l], ffn2_w_up[l], ffn2_w_down[l])
    return x
```

```python
import functools

import jax
import jax.numpy as jnp
from jax import lax
from jax.experimental import pallas as pl
from jax.experimental.pallas import tpu as pltpu

F32 = jnp.float32
BF16 = jnp.bfloat16

D_MODEL = 2048
D_FF = 5632
EPS = 1e-6
MASK_VALUE = -1e30
FORCE_SCORE = 1e4

NSA_HEADS = 16
KV_GROUPS = 2
HEAD_DIM = 64
NSA_WIDTH = NSA_HEADS * HEAD_DIM
GROUP_WIDTH = NSA_WIDTH // KV_GROUPS
HEAD_PAIRS = GROUP_WIDTH // 128
N_BRANCH = 3
CMP_BLOCK = 32
CMP_STRIDE = 16
CMP_HIDDEN = 4 * HEAD_DIM
SEL_BLOCK = 64
N_SEL = 8
WINDOW = 512
Q_BLOCK = 128
CONV_WIDTH = 1024
CONV_K = 3
XA_HEADS = 4
XA_HEAD_DIM = 128
XA_WIDTH = XA_HEADS * XA_HEAD_DIM

LANES = 128
KEY_TILE = 128
N_CMP_PAD = 128

COL_Q = 0
COL_KV = NSA_WIDTH
COL_GATE = COL_KV + 6 * LANES
COL_CONV = COL_GATE + KV_GROUPS * LANES
PROJ_COLS = COL_CONV + 3 * CONV_WIDTH

VMEM_LIMIT = 56 * 1024 * 1024


def _cparams(sem):
    return pltpu.CompilerParams(dimension_semantics=sem, vmem_limit_bytes=VMEM_LIMIT)


def _rms(x, g):
    ms = jnp.mean(x * x, axis=-1, keepdims=True)
    return x * lax.rsqrt(ms + EPS) * g


def _dot(a, b):
    return jnp.dot(a, b, preferred_element_type=F32)


def _dot_nt(a, b):
    return lax.dot_general(a, b, (((1,), (1,)), ((), ())), preferred_element_type=F32)


def _ffn_kernel(x_ref, g_ref, wg_ref, wu_ref, wd_ref, o_ref, h_ref):
    j = pl.program_id(1)

    @pl.when(j == 0)
    def _():
        h_ref[...] = _rms(x_ref[...], g_ref[...]).astype(BF16)

    h = h_ref[...]
    a = _dot(h, wg_ref[...])
    u = _dot(h, wu_ref[...])
    act = (a * jax.nn.sigmoid(a) * u).astype(BF16)
    part = _dot(act, wd_ref[...])

    @pl.when(j == 0)
    def _():
        o_ref[...] = part

    @pl.when(j > 0)
    def _():
        o_ref[...] += part

    @pl.when(j == pl.num_programs(1) - 1)
    def _():
        o_ref[...] = x_ref[...] + 0.5 * o_ref[...]


def _ffn(x, g, wg, wu, wd, layer, tm=512, tf=512):
    T = x.shape[0]
    return pl.pallas_call(
        _ffn_kernel,
        grid=(T // tm, D_FF // tf),
        in_specs=[
            pl.BlockSpec((tm, D_MODEL), lambda i, j: (i, 0)),
            pl.BlockSpec((None, 1, D_MODEL), lambda i, j: (layer, 0, 0)),
            pl.BlockSpec((None, D_MODEL, tf), lambda i, j: (layer, 0, j)),
            pl.BlockSpec((None, D_MODEL, tf), lambda i, j: (layer, 0, j)),
            pl.BlockSpec((None, tf, D_MODEL), lambda i, j: (layer, j, 0)),
        ],
        out_specs=pl.BlockSpec((tm, D_MODEL), lambda i, j: (i, 0)),
        out_shape=jax.ShapeDtypeStruct((T, D_MODEL), F32),
        scratch_shapes=[pltpu.VMEM((tm, D_MODEL), BF16)],
        compiler_params=_cparams(("parallel", "arbitrary")),
        name="ffn",
    )(x, g, wg, wu, wd)


def _norm_matmul_kernel(x_ref, g_ref, w_ref, o_ref, h_ref):
    @pl.when(pl.program_id(1) == 0)
    def _():
        h_ref[...] = _rms(x_ref[...], g_ref[...]).astype(BF16)

    o_ref[...] = _dot(h_ref[...], w_ref[...])


def _norm_matmul(x, g, w, tm, tn, name):
    T, K = x.shape
    N = w.shape[1]
    return pl.pallas_call(
        _norm_matmul_kernel,
        grid=(T // tm, N // tn),
        in_specs=[
            pl.BlockSpec((tm, K), lambda i, j: (i, 0)),
            pl.BlockSpec((1, K), lambda i, j: (0, 0)),
            pl.BlockSpec((K, tn), lambda i, j: (0, j)),
        ],
        out_specs=pl.BlockSpec((tm, tn), lambda i, j: (i, j)),
        out_shape=jax.ShapeDtypeStruct((T, N), F32),
        scratch_shapes=[pltpu.VMEM((tm, K), BF16)],
        compiler_params=_cparams(("parallel", "arbitrary")),
        name=name,
    )(x, g, w)


def _half_rms(x, lo, gain):
    sq = x * x
    s_lo = jnp.sum(jnp.where(lo, sq, 0.0), axis=-1, keepdims=True)
    s_hi = jnp.sum(jnp.where(lo, 0.0, sq), axis=-1, keepdims=True)
    inv = jnp.where(lo, lax.rsqrt(s_lo * (1.0 / HEAD_DIM) + EPS),
                    lax.rsqrt(s_hi * (1.0 / HEAD_DIM) + EPS))
    return x * inv * gain


def _prep_kernel(ck_ref, cv_ref, s_ref, w_ref, pek_ref, w1k_ref, w2k_ref, pev_ref, w1v_ref, w2v_ref,
                 kn_ref, kc_o, vc_o, ks_o, vs_o, kw_o, vw_o):
    S = cv_ref.shape[0]
    n_tiles = S // KEY_TILE
    lo = lax.broadcasted_iota(jnp.int32, (1, LANES), 1) < HEAD_DIM
    half = CMP_BLOCK // 2

    def compress(src_ref, pe_ref, w1_ref, w2_ref):
        acc_a = jnp.zeros((2 * N_CMP_PAD, CMP_HIDDEN), F32)
        acc_b = jnp.zeros((2 * N_CMP_PAD, CMP_HIDDEN), F32)
        for l in range(half):
            x = src_ref[pl.ds(l, N_CMP_PAD, stride=CMP_STRIDE), :]
            for acc_is_b, ll in ((False, l), (True, l + half)):
                xp = x + pe_ref[ll:ll + 1, :]
                x2 = jnp.concatenate([jnp.where(lo, xp, 0.0), jnp.where(lo, 0.0, xp)],
                                     axis=0).astype(BF16)
                d = _dot(x2, w1_ref[ll])
                if acc_is_b:
                    acc_b = acc_b + d
                else:
                    acc_a = acc_a + d
        hidden = acc_a + pltpu.roll(acc_b, 2 * N_CMP_PAD - 1, 0)
        act = jax.nn.gelu(hidden, approximate=True).astype(BF16)
        return _dot(act, w2_ref[...])

    kc = _half_rms(compress(ck_ref, pek_ref, w1k_ref, w2k_ref), lo, kn_ref[0:1, :])
    vc = compress(cv_ref, pev_ref, w1v_ref, w2v_ref)
    for g in range(KV_GROUPS):
        rows = slice(g * N_CMP_PAD, (g + 1) * N_CMP_PAD)
        kc_o[g, 0:N_CMP_PAD, :] = jnp.where(lo, kc[rows], 0.0).astype(BF16)
        kc_o[g, N_CMP_PAD:, :] = jnp.where(lo, 0.0, kc[rows]).astype(BF16)
        vc_o[g, 0:N_CMP_PAD, :] = jnp.where(lo, vc[rows], 0.0).astype(BF16)
        vc_o[g, N_CMP_PAD:, :] = jnp.where(lo, 0.0, vc[rows]).astype(BF16)

    def emit(src_ref, gain, k_o, v_o):
        k = _half_rms(src_ref[:, 0:LANES], lo, gain)
        v = src_ref[:, LANES:2 * LANES]
        k_sw = pltpu.roll(k, HEAD_DIM, 1)
        v_sw = pltpu.roll(v, HEAD_DIM, 1)
        ones_lo = jnp.where(lo, 1.0, 0.0)
        ones_hi = jnp.where(lo, 0.0, 1.0)
        for g in range(KV_GROUPS):
            k_lo, k_hi = (k, k_sw) if g == 0 else (k_sw, k)
            v_lo, v_hi = (v, v_sw) if g == 0 else (v_sw, v)
            shape3 = (n_tiles, KEY_TILE, LANES)
            k_o[g, :, 0:KEY_TILE, :] = jnp.where(lo, k_lo, 0.0).astype(BF16).reshape(shape3)
            k_o[g, :, KEY_TILE:, :] = jnp.where(lo, 0.0, k_hi).astype(BF16).reshape(shape3)
            v_o[g, :, 0:KEY_TILE, 0:LANES] = jnp.where(lo, v_lo, 0.0).astype(BF16).reshape(shape3)
            v_o[g, :, KEY_TILE:, 0:LANES] = jnp.where(lo, 0.0, v_hi).astype(BF16).reshape(shape3)
            v_o[g, :, 0:KEY_TILE, LANES:] = jnp.broadcast_to(ones_lo, (S, LANES)).astype(BF16).reshape(shape3)
            v_o[g, :, KEY_TILE:, LANES:] = jnp.broadcast_to(ones_hi, (S, LANES)).astype(BF16).reshape(shape3)

    emit(s_ref, kn_ref[1:2, :], ks_o, vs_o)
    emit(w_ref, kn_ref[2:3, :], kw_o, vw_o)


def _prep(proj, pek, w1k, w2k, pev, w1v, w2v, kn, B, S):
    n_tiles = S // KEY_TILE
    kv_blk = COL_KV // 256
    full = lambda shape: pl.BlockSpec(shape, lambda b: (0,) * len(shape))
    cmp_shape = jax.ShapeDtypeStruct((B, KV_GROUPS, 2 * N_CMP_PAD, LANES), BF16)
    k_shape = jax.ShapeDtypeStruct((B, KV_GROUPS, n_tiles, 2 * KEY_TILE, LANES), BF16)
    v_shape = jax.ShapeDtypeStruct((B, KV_GROUPS, n_tiles, 2 * KEY_TILE, 2 * LANES), BF16)
    cmp_spec = pl.BlockSpec((None, KV_GROUPS, 2 * N_CMP_PAD, LANES), lambda b: (b, 0, 0, 0))
    k_spec = pl.BlockSpec((None, KV_GROUPS, n_tiles, 2 * KEY_TILE, LANES), lambda b: (b, 0, 0, 0, 0))
    v_spec = pl.BlockSpec((None, KV_GROUPS, n_tiles, 2 * KEY_TILE, 2 * LANES), lambda b: (b, 0, 0, 0, 0))
    return pl.pallas_call(
        _prep_kernel,
        grid=(B,),
        in_specs=[
            pl.BlockSpec((S, LANES), lambda b: (b, COL_KV // LANES)),
            pl.BlockSpec((S, LANES), lambda b: (b, COL_KV // LANES + 1)),
            pl.BlockSpec((S, 256), lambda b: (b, kv_blk + 1)),
            pl.BlockSpec((S, 256), lambda b: (b, kv_blk + 2)),
            full(pek.shape), full(w1k.shape), full(w2k.shape),
            full(pev.shape), full(w1v.shape), full(w2v.shape), full(kn.shape),
        ],
        out_specs=[cmp_spec, cmp_spec, k_spec, v_spec, k_spec, v_spec],
        out_shape=[cmp_shape, cmp_shape, k_shape, v_shape, k_shape, v_shape],
        compiler_params=_cparams(("parallel",)),
        name="kv_prep",
    )(proj, proj, proj, proj, pek, w1k, w2k, pev, w1v, w2v, kn)


def _nsa_kernel(q_ref, gate_ref, qn_ref, kc_ref, vc_ref, ks_ref, vs_ref, kw_ref, vw_ref, e_ref,
                o_ref, qp_ref, pc_ref, acc_ref, m_ref, bias_ref):
    i = pl.program_id(2)
    n_tiles = ks_ref.shape[0]
    lane = lax.broadcasted_iota(jnp.int32, (Q_BLOCK, LANES), 1)
    row = lax.broadcasted_iota(jnp.int32, (Q_BLOCK, LANES), 0)
    lo = lane < HEAD_DIM
    t = i * Q_BLOCK + row
    rows_all = HEAD_PAIRS * Q_BLOCK

    scale = HEAD_DIM ** -0.5
    for c in range(HEAD_PAIRS):
        x = q_ref[:, c * LANES:(c + 1) * LANES]
        qp_ref[c * Q_BLOCK:(c + 1) * Q_BLOCK, :] = (_half_rms(x, lo, qn_ref[...]) * scale).astype(BF16)
    qp = qp_ref[...]

    sc = _dot_nt(qp, kc_ref[...])
    valid_c = lane * CMP_STRIDE + (CMP_BLOCK - 1) <= t
    psum = jnp.zeros((Q_BLOCK, LANES), F32)
    for c in range(HEAD_PAIRS):
        for hf in range(2):
            s = sc[c * Q_BLOCK:(c + 1) * Q_BLOCK, hf * LANES:(hf + 1) * LANES]
            s = jnp.where(valid_c, s, MASK_VALUE)
            e = jnp.where(valid_c, jnp.exp(s - jnp.max(s, axis=-1, keepdims=True)), 0.0)
            den = jnp.sum(e, axis=-1, keepdims=True)
            p = e / jnp.where(den > 0.0, den, 1.0)
            psum = psum + p
            pc_ref[c * Q_BLOCK:(c + 1) * Q_BLOCK, hf * LANES:(hf + 1) * LANES] = p.astype(BF16)
    o_cmp = _dot(pc_ref[...], vc_ref[...])

    imp = (psum + pltpu.roll(psum, 1, 1) + pltpu.roll(psum, LANES - 1, 1)
           + pltpu.roll(psum, LANES - 2, 1) + pltpu.roll(psum, LANES - 3, 1))
    ratio = SEL_BLOCK // CMP_STRIDE
    blk = lane // ratio
    is_blk = (lane % ratio) == 0
    cur = t // SEL_BLOCK
    forced = (blk == 0) | (blk == cur) | (blk == cur - 1)
    score = jnp.where(blk > cur, -1.0, jnp.where(forced, FORCE_SCORE, imp))
    score = jnp.where(is_blk, score, -2.0)
    rank = jnp.zeros((Q_BLOCK, LANES), F32)
    for j in range(LANES // ratio):
        col = jnp.broadcast_to(score[:, ratio * j:ratio * j + 1], (Q_BLOCK, LANES))
        beats = (col > score) | ((col == score) & (lane > ratio * j))
        rank = rank + jnp.where(beats, 1.0, 0.0)
    sel = jnp.where((rank < float(N_SEL)) & is_blk, 1.0, 0.0).astype(BF16)
    chosen = _dot(sel, e_ref[...])
    for kt in range(n_tiles):
        ok = (chosen[:, kt * KEY_TILE:(kt + 1) * KEY_TILE] > 0.5) & (kt * KEY_TILE + lane <= t)
        bias_ref[kt] = jnp.where(ok, 0.0, MASK_VALUE)

    def attend(k_ref, v_ref, first_tile, bias_fn):
        m_ref[...] = jnp.full(m_ref.shape, MASK_VALUE, F32)
        acc_ref[...] = jnp.zeros(acc_ref.shape, F32)

        def body(kt, carry):
            s = _dot_nt(qp, k_ref[kt])
            bias = bias_fn(kt)
            bias2 = jnp.concatenate([bias, bias], axis=1)
            s = (s.reshape(HEAD_PAIRS, Q_BLOCK, 2 * LANES) + bias2[None]).reshape(rows_all, 2 * LANES)
            m_prev = m_ref[...]
            m_e = jnp.max(s[:, :LANES], axis=-1, keepdims=True)
            m_o = jnp.max(s[:, LANES:], axis=-1, keepdims=True)
            m_cur = jnp.concatenate([jnp.broadcast_to(m_e, (rows_all, LANES)),
                                     jnp.broadcast_to(m_o, (rows_all, LANES))], axis=1)
            m_new = jnp.maximum(m_prev, m_cur)
            p = jnp.exp(s - m_new).astype(BF16)
            lo4 = jnp.concatenate([lo] * HEAD_PAIRS, axis=0)
            pair = lambda mm: jnp.where(lo4, mm[:, :LANES], mm[:, LANES:])
            alpha = jnp.exp(pair(m_prev) - pair(m_new))
            alpha2 = jnp.concatenate([alpha, alpha], axis=1)
            acc_ref[...] = alpha2 * acc_ref[...] + _dot(p, v_ref[kt])
            m_ref[...] = m_new
            return carry

        lax.fori_loop(first_tile, i + 1, body, 0)
        acc = acc_ref[...]
        return acc[:, :LANES] / acc[:, LANES:]

    o_sel = attend(ks_ref, vs_ref, 0, lambda kt: bias_ref[kt])

    def window_bias(kt):
        lag = t - (kt * KEY_TILE + lane)
        return jnp.where((lag >= 0) & (lag < WINDOW), 0.0, MASK_VALUE)

    o_win = attend(kw_ref, vw_ref, jnp.maximum(i - WINDOW // KEY_TILE, 0), window_bias)

    gates = jax.nn.sigmoid(gate_ref[...])
    heads_per_group = NSA_HEADS // KV_GROUPS

    def gate_pair(br, c):
        col = br * heads_per_group + 2 * c
        g_e = jnp.broadcast_to(gates[:, col:col + 1], (Q_BLOCK, LANES))
        g_o = jnp.broadcast_to(gates[:, col + 1:col + 2], (Q_BLOCK, LANES))
        return jnp.where(lo, g_e, g_o)

    for c in range(HEAD_PAIRS):
        rows = slice(c * Q_BLOCK, (c + 1) * Q_BLOCK)
        o_ref[:, c * LANES:(c + 1) * LANES] = (gate_pair(0, c) * o_cmp[rows]
                                               + gate_pair(1, c) * o_sel[rows]
                                               + gate_pair(2, c) * o_win[rows])


def _nsa(proj, qn, kc, vc, ks, vs, kw, vw, expand, B, S):
    n_q = S // Q_BLOCK
    n_tiles = S // KEY_TILE
    gate_blk = COL_GATE // LANES
    cmp_spec = pl.BlockSpec((None, None, 2 * N_CMP_PAD, LANES), lambda b, g, i: (b, g, 0, 0))
    k_spec = pl.BlockSpec((None, None, n_tiles, 2 * KEY_TILE, LANES), lambda b, g, i: (b, g, 0, 0, 0))
    v_spec = pl.BlockSpec((None, None, n_tiles, 2 * KEY_TILE, 2 * LANES), lambda b, g, i: (b, g, 0, 0, 0))
    rows_all = HEAD_PAIRS * Q_BLOCK
    return pl.pallas_call(
        _nsa_kernel,
        grid=(B, KV_GROUPS, n_q),
        in_specs=[
            pl.BlockSpec((Q_BLOCK, GROUP_WIDTH), lambda b, g, i: (b * n_q + i, g)),
            pl.BlockSpec((Q_BLOCK, LANES), lambda b, g, i: (b * n_q + i, gate_blk + g)),
            pl.BlockSpec((1, LANES), lambda b, g, i: (0, 0)),
            cmp_spec, cmp_spec, k_spec, v_spec, k_spec, v_spec,
            pl.BlockSpec((LANES, S), lambda b, g, i: (0, 0)),
        ],
        out_specs=pl.BlockSpec((Q_BLOCK, GROUP_WIDTH), lambda b, g, i: (b * n_q + i, g)),
        out_shape=jax.ShapeDtypeStruct((B * S, NSA_WIDTH), F32),
        scratch_shapes=[
            pltpu.VMEM((rows_all, LANES), BF16),
            pltpu.VMEM((rows_all, 2 * LANES), BF16),
            pltpu.VMEM((rows_all, 2 * LANES), F32),
            pltpu.VMEM((rows_all, 2 * LANES), F32),
            pltpu.VMEM((n_tiles, Q_BLOCK, KEY_TILE), F32),
        ],
        compiler_params=_cparams(("parallel", "parallel", "arbitrary")),
        name="nsa_attention",
    )(proj, proj, qn, kc, vc, ks, vs, kw, vw, expand)


def _conv_kernel(b_ref, c_ref, x_ref, w_ref, o_ref):
    u = c_ref[...] * x_ref[...]
    row = lax.broadcasted_iota(jnp.int32, u.shape, 0)
    u1 = jnp.where(row >= 1, pltpu.roll(u, 1, 0), 0.0)
    u2 = jnp.where(row >= 2, pltpu.roll(u, 2, 0), 0.0)
    y = w_ref[2:3, :] * u + w_ref[1:2, :] * u1 + w_ref[0:1, :] * u2
    o_ref[...] = b_ref[...] * y


def _conv(proj, conv_w, B, S, cw=256):
    nb = CONV_WIDTH // cw
    base = COL_CONV // cw
    return pl.pallas_call(
        _conv_kernel,
        grid=(B, nb),
        in_specs=[
            pl.BlockSpec((S, cw), lambda b, j: (b, base + j)),
            pl.BlockSpec((S, cw), lambda b, j: (b, base + nb + j)),
            pl.BlockSpec((S, cw), lambda b, j: (b, base + 2 * nb + j)),
            pl.BlockSpec((CONV_K, cw), lambda b, j: (0, j)),
        ],
        out_specs=pl.BlockSpec((S, cw), lambda b, j: (b, j)),
        out_shape=jax.ShapeDtypeStruct((B * S, CONV_WIDTH), F32),
        compiler_params=_cparams(("parallel", "parallel")),
        name="gated_conv",
    )(proj, proj, proj, conv_w)


def _mix_out_kernel(x_ref, a_ref, b_ref, ga_ref, gb_ref, wa_ref, wb_ref, o_ref):
    a = _rms(a_ref[...], ga_ref[...]).astype(BF16)
    b = _rms(b_ref[...], gb_ref[...]).astype(BF16)
    o_ref[...] = x_ref[...] + _dot(a, wa_ref[...]) + _dot(b, wb_ref[...])


def _mix_out(x, oa, ob, ga, gb, wa, wb, tm=256):
    T = x.shape[0]
    const = lambda shape: pl.BlockSpec(shape, lambda i: (0, 0))
    return pl.pallas_call(
        _mix_out_kernel,
        grid=(T // tm,),
        in_specs=[
            pl.BlockSpec((tm, D_MODEL), lambda i: (i, 0)),
            pl.BlockSpec((tm, NSA_WIDTH), lambda i: (i, 0)),
            pl.BlockSpec((tm, CONV_WIDTH), lambda i: (i, 0)),
            const((1, NSA_WIDTH)), const((1, CONV_WIDTH)),
            const((NSA_WIDTH, D_MODEL)), const((CONV_WIDTH, D_MODEL)),
        ],
        out_specs=pl.BlockSpec((tm, D_MODEL), lambda i: (i, 0)),
        out_shape=jax.ShapeDtypeStruct((T, D_MODEL), F32),
        compiler_params=_cparams(("parallel",)),
        name="mixer_out",
    )(x, oa, ob, ga, gb, wa, wb)


def _xattn_kernel(x_ref, g_ref, wq_ref, qn_ref, kn_ref, kv_ref, wo_ref, o_ref, oh_ref):
    x = x_ref[...]
    h = _rms(x, g_ref[...]).astype(BF16)
    q = _dot(h, wq_ref[...])
    scale = XA_HEAD_DIM ** -0.5
    for hd in range(XA_HEADS):
        cols = slice(hd * XA_HEAD_DIM, (hd + 1) * XA_HEAD_DIM)
        qh = _rms(q[:, cols], qn_ref[...]).astype(BF16)
        kh = _rms(kv_ref[:, cols], kn_ref[...]).astype(BF16)
        vh = kv_ref[:, XA_WIDTH + hd * XA_HEAD_DIM:XA_WIDTH + (hd + 1) * XA_HEAD_DIM].astype(BF16)
        s = _dot_nt(qh, kh) * scale
        e = jnp.exp(s - jnp.max(s, axis=-1, keepdims=True))
        p = (e / jnp.sum(e, axis=-1, keepdims=True)).astype(BF16)
        oh_ref[:, cols] = _dot(p, vh).astype(BF16)
    o_ref[...] = x + _dot(oh_ref[...], wo_ref[...])


def _xattn(x, g, wq, qn, kn, kv, wo, B, S, tm=256):
    M = kv.shape[0] // B
    n_t = S // tm
    const = lambda shape: pl.BlockSpec(shape, lambda b, i: (0, 0))
    return pl.pallas_call(
        _xattn_kernel,
        grid=(B, n_t),
        in_specs=[
            pl.BlockSpec((tm, D_MODEL), lambda b, i: (b * n_t + i, 0)),
            const((1, D_MODEL)), const((D_MODEL, XA_WIDTH)),
            const((1, XA_HEAD_DIM)), const((1, XA_HEAD_DIM)),
            pl.BlockSpec((M, 2 * XA_WIDTH), lambda b, i: (b, 0)),
            const((XA_WIDTH, D_MODEL)),
        ],
        out_specs=pl.BlockSpec((tm, D_MODEL), lambda b, i: (b * n_t + i, 0)),
        out_shape=jax.ShapeDtypeStruct((B * S, D_MODEL), F32),
        scratch_shapes=[pltpu.VMEM((tm, XA_WIDTH), BF16)],
        compiler_params=_cparams(("parallel", "parallel")),
        name="mem_xattn",
    )(x, g, wq, qn, kn, kv, wo)


def _dup(v):
    return jnp.concatenate([v, v], axis=-1)


def _reorder_w_in(w_in):
    L = w_in.shape[0]
    n_gate = N_BRANCH * NSA_HEADS
    heads_per_group = NSA_HEADS // KV_GROUPS
    kv_end = NSA_WIDTH + 6 * LANES
    gate = w_in[:, :, kv_end:kv_end + n_gate]
    gate = gate.reshape(L, D_MODEL, N_BRANCH, KV_GROUPS, heads_per_group).transpose(0, 1, 3, 2, 4)
    gate = gate.reshape(L, D_MODEL, KV_GROUPS, N_BRANCH * heads_per_group)
    gate = jnp.pad(gate, ((0, 0), (0, 0), (0, 0), (0, LANES - N_BRANCH * heads_per_group)))
    gate = gate.reshape(L, D_MODEL, KV_GROUPS * LANES)
    out = jnp.concatenate([w_in[:, :, :kv_end], gate, w_in[:, :, kv_end + n_gate:]], axis=-1)
    return out.astype(BF16)


def _selection_expand(S):
    ratio = SEL_BLOCK // CMP_STRIDE
    r = jnp.arange(LANES)[:, None]
    k = jnp.arange(S)[None, :]
    return ((r % ratio == 0) & (k // SEL_BLOCK == r // ratio)).astype(BF16)


def kernel(x, mem, ffn1_norm, ffn1_w_gate, ffn1_w_up, ffn1_w_down, mix_norm, w_in, cmp_pe_k, cmp_w1_k, cmp_w2_k, cmp_pe_v, cmp_w1_v, cmp_w2_v, q_norm, k_norm, conv_w, out_norm_nsa, out_norm_conv, w_out, xattn_norm, mem_norm, xattn_w_q, xattn_w_kv, xattn_q_norm, xattn_k_norm, xattn_w_o, ffn2_norm, ffn2_w_gate, ffn2_w_up, ffn2_w_down):
    B, S, D = x.shape
    L = w_in.shape[0]
    T = B * S
    M = mem.shape[1]
    bf = lambda w: w.astype(BF16)

    f1g, f1u, f1d = bf(ffn1_w_gate), bf(ffn1_w_up), bf(ffn1_w_down)
    f2g, f2u, f2d = bf(ffn2_w_gate), bf(ffn2_w_up), bf(ffn2_w_down)
    w_in_r = _reorder_w_in(w_in)
    w_out_b = bf(w_out)
    wq_b, wkv_b, wo_b = bf(xattn_w_q), bf(xattn_w_kv), bf(xattn_w_o)
    w1k = bf(_dup(cmp_w1_k.reshape(L, CMP_BLOCK, HEAD_DIM, CMP_HIDDEN).swapaxes(2, 3)).swapaxes(2, 3))
    w1v = bf(_dup(cmp_w1_v.reshape(L, CMP_BLOCK, HEAD_DIM, CMP_HIDDEN).swapaxes(2, 3)).swapaxes(2, 3))
    w2k, w2v = bf(_dup(cmp_w2_k)), bf(_dup(cmp_w2_v))
    pek, pev = _dup(cmp_pe_k), _dup(cmp_pe_v)
    qn, kn = _dup(q_norm), _dup(k_norm)
    expand = _selection_expand(S)

    r3 = lambda a: a.reshape(L, 1, a.shape[-1])
    f1n, f2n = r3(ffn1_norm), r3(ffn2_norm)

    xs = x.reshape(T, D)
    mem2 = mem.reshape(B * M, D)
    for l in range(L):
        xs = _ffn(xs, f1n, f1g, f1u, f1d, l)
        proj = _norm_matmul(xs, mix_norm[l][None], w_in_r[l], 512, 1024, "mixer_in")
        kc, vc, ks, vs, kw, vw = _prep(proj, pek[l], w1k[l], w2k[l], pev[l], w1v[l], w2v[l], kn[l], B, S)
        o_a = _nsa(proj, qn[l][None], kc, vc, ks, vs, kw, vw, expand, B, S)
        o_b = _conv(proj, conv_w[l], B, S)
        xs = _mix_out(xs, o_a, o_b, out_norm_nsa[l][None], out_norm_conv[l][None],
                      w_out_b[l, :NSA_WIDTH], w_out_b[l, NSA_WIDTH:])
        kv = _norm_matmul(mem2, mem_norm[l][None], wkv_b[l], 512, 1024, "mem_kv")
        xs = _xattn(xs, xattn_norm[l][None], wq_b[l], xattn_q_norm[l][None], xattn_k_norm[l][None],
                    kv, wo_b[l], B, S)
        xs = _ffn(xs, f2n, f2g, f2u, f2d, l)
    return xs.reshape(B, S, D)
```

```python
import math

import jax
import jax.numpy as jnp
from jax import lax
from jax.experimental import pallas as pl
from jax.experimental.pallas import tpu as pltpu

F32 = jnp.float32
BF16 = jnp.bfloat16

D_MODEL = 2048
D_FF = 5632
EPS = 1e-6
MASK_VALUE = -1e30
FORCE_SCORE = 1e4
LOG2E = math.log2(math.e)

NSA_HEADS = 16
KV_GROUPS = 2
HEADS_PER_GROUP = NSA_HEADS // KV_GROUPS
HEAD_DIM = 64
NSA_WIDTH = NSA_HEADS * HEAD_DIM
GROUP_WIDTH = NSA_WIDTH // KV_GROUPS
HEAD_PAIRS = GROUP_WIDTH // 128
N_BRANCH = 3
CMP_BLOCK = 32
CMP_STRIDE = 16
CMP_HIDDEN = 4 * HEAD_DIM
SEL_BLOCK = 64
N_SEL = 8
WINDOW = 512
Q_BLOCK = 128
CONV_WIDTH = 1024
CONV_K = 3
XA_HEADS = 4
XA_HEAD_DIM = 128
XA_WIDTH = XA_HEADS * XA_HEAD_DIM

LANES = 128
SUBLANES = 8
KEY_TILE = 128
N_CMP_PAD = 128
CMP_PER_SEL = SEL_BLOCK // CMP_STRIDE

COL_Q = 0
COL_KV = NSA_WIDTH
COL_GATE = COL_KV + 6 * LANES
COL_CONV = COL_GATE + KV_GROUPS * LANES
PROJ_COLS = COL_CONV + 3 * CONV_WIDTH

VMEM_LIMIT = 56 * 1024 * 1024


def _cparams(sem):
    return pltpu.CompilerParams(dimension_semantics=sem, vmem_limit_bytes=VMEM_LIMIT)


def _rms(x, g):
    ms = jnp.mean(x * x, axis=-1, keepdims=True)
    return x * lax.rsqrt(ms + EPS) * g


def _dot(a, b):
    return jnp.dot(a, b, preferred_element_type=F32)


def _dot_nt(a, b):
    return lax.dot_general(a, b, (((1,), (1,)), ((), ())), preferred_element_type=F32)


def _layer_spec(tail_shape, layer, tail_index):
    return pl.BlockSpec((None,) + tuple(tail_shape), lambda *g: (layer,) + tuple(tail_index(*g)))


FFN_SUB = 256


def _ffn_kernel(x_ref, g_ref, wg_ref, wu_ref, wd_ref, o_ref, h_ref):
    j = pl.program_id(1)

    @pl.when(j == 0)
    def _():
        h_ref[...] = _rms(x_ref[...], g_ref[...]).astype(BF16)
        o_ref[...] = jnp.zeros(o_ref.shape, F32)

    h = h_ref[...]
    part = None
    for c in range(wg_ref.shape[1] // FFN_SUB):
        cols = slice(c * FFN_SUB, (c + 1) * FFN_SUB)
        a = _dot(h, wg_ref[:, cols])
        u = _dot(h, wu_ref[:, cols])
        act = (a * jax.nn.sigmoid(a) * u).astype(BF16)
        d = _dot(act, wd_ref[cols, :])
        part = d if part is None else part + d
    o_ref[...] += part

    @pl.when(j == pl.num_programs(1) - 1)
    def _():
        o_ref[...] = x_ref[...] + 0.5 * o_ref[...]


def _ffn(x, g, wg, wu, wd, layer, tm=512, tf=512):
    T = x.shape[0]
    return pl.pallas_call(
        _ffn_kernel,
        grid=(T // tm, D_FF // tf),
        in_specs=[
            pl.BlockSpec((tm, D_MODEL), lambda i, j: (i, 0)),
            _layer_spec((1, D_MODEL), layer, lambda i, j: (0, 0)),
            _layer_spec((D_MODEL, tf), layer, lambda i, j: (0, j)),
            _layer_spec((D_MODEL, tf), layer, lambda i, j: (0, j)),
            _layer_spec((tf, D_MODEL), layer, lambda i, j: (j, 0)),
        ],
        out_specs=pl.BlockSpec((tm, D_MODEL), lambda i, j: (i, 0)),
        out_shape=jax.ShapeDtypeStruct((T, D_MODEL), F32),
        scratch_shapes=[pltpu.VMEM((tm, D_MODEL), BF16)],
        compiler_params=_cparams(("parallel", "arbitrary")),
        name="ffn",
    )(x, g, wg, wu, wd)


def _norm_matmul_kernel(x_ref, g_ref, w_ref, o_ref, h_ref):
    @pl.when(pl.program_id(1) == 0)
    def _():
        h_ref[...] = _rms(x_ref[...], g_ref[...]).astype(BF16)

    o_ref[...] = _dot(h_ref[...], w_ref[...])


def _norm_matmul(x, g, w, layer, tm, tn, name):
    T, K = x.shape
    N = w.shape[2]
    return pl.pallas_call(
        _norm_matmul_kernel,
        grid=(T // tm, N // tn),
        in_specs=[
            pl.BlockSpec((tm, K), lambda i, j: (i, 0)),
            _layer_spec((1, K), layer, lambda i, j: (0, 0)),
            _layer_spec((K, tn), layer, lambda i, j: (0, j)),
        ],
        out_specs=pl.BlockSpec((tm, tn), lambda i, j: (i, j)),
        out_shape=jax.ShapeDtypeStruct((T, N), F32),
        scratch_shapes=[pltpu.VMEM((tm, K), BF16)],
        compiler_params=_cparams(("parallel", "arbitrary")),
        name=name,
    )(x, g, w)


def _half_rms(x, lo, gain):
    sq = x * x
    s_lo = jnp.sum(jnp.where(lo, sq, 0.0), axis=-1, keepdims=True)
    s_hi = jnp.sum(jnp.where(lo, 0.0, sq), axis=-1, keepdims=True)
    inv = jnp.where(lo, lax.rsqrt(s_lo * (1.0 / HEAD_DIM) + EPS),
                    lax.rsqrt(s_hi * (1.0 / HEAD_DIM) + EPS))
    return x * inv * gain


def _prep_kernel(ck_ref, cv_ref, s_ref, w_ref, pek_ref, w1k_ref, w2k_ref, pev_ref, w1v_ref, w2v_ref,
                 kn_ref, kc_o, vc_o, ks_o, vs_o, kw_o, vw_o):
    S = cv_ref.shape[0]
    n_tiles = S // KEY_TILE
    lo = lax.broadcasted_iota(jnp.int32, (1, LANES), 1) < HEAD_DIM
    half = CMP_BLOCK // 2

    def compress(src_ref, pe_ref, w1_ref, w2_ref):
        acc_a = jnp.zeros((2 * N_CMP_PAD, CMP_HIDDEN), F32)
        acc_b = jnp.zeros((2 * N_CMP_PAD, CMP_HIDDEN), F32)
        for l in range(half):
            x = src_ref[pl.ds(l, N_CMP_PAD, stride=CMP_STRIDE), :]
            for acc_is_b, ll in ((False, l), (True, l + half)):
                xp = x + pe_ref[ll:ll + 1, :]
                x2 = jnp.concatenate([jnp.where(lo, xp, 0.0), jnp.where(lo, 0.0, xp)],
                                     axis=0).astype(BF16)
                d = _dot(x2, w1_ref[ll])
                if acc_is_b:
                    acc_b = acc_b + d
                else:
                    acc_a = acc_a + d
        hidden = acc_a + pltpu.roll(acc_b, 2 * N_CMP_PAD - 1, 0)
        act = jax.nn.gelu(hidden, approximate=True).astype(BF16)
        return _dot(act, w2_ref[...])

    kc = _half_rms(compress(ck_ref, pek_ref, w1k_ref, w2k_ref), lo, kn_ref[0:1, :])
    vc = compress(cv_ref, pev_ref, w1v_ref, w2v_ref)
    top = lax.broadcasted_iota(jnp.int32, (LANES, 1), 0) < HEAD_DIM
    for g in range(KV_GROUPS):
        rows = slice(g * N_CMP_PAD, (g + 1) * N_CMP_PAD)
        kc_o[g, 0:N_CMP_PAD, :] = jnp.where(lo, kc[rows], 0.0).astype(BF16)
        kc_o[g, N_CMP_PAD:, :] = jnp.where(lo, 0.0, kc[rows]).astype(BF16)
        vt = vc[rows].T
        vc_o[g, :, 0:N_CMP_PAD] = jnp.where(top, vt, 0.0).astype(BF16)
        vc_o[g, :, N_CMP_PAD:] = jnp.where(top, 0.0, vt).astype(BF16)

    def emit(src_ref, gain, k_o, v_o):
        k = _half_rms(src_ref[:, 0:LANES], lo, gain)
        k_sw = pltpu.roll(k, HEAD_DIM, 1)
        vt = src_ref[:, LANES:2 * LANES].T.astype(BF16)
        zeros = jnp.zeros((HEAD_DIM, KEY_TILE), BF16)
        for g in range(KV_GROUPS):
            k_lo, k_hi = (k, k_sw) if g == 0 else (k_sw, k)
            shape3 = (n_tiles, KEY_TILE, LANES)
            k_o[g, :, 0:KEY_TILE, :] = jnp.where(lo, k_lo, 0.0).astype(BF16).reshape(shape3)
            k_o[g, :, KEY_TILE:, :] = jnp.where(lo, 0.0, k_hi).astype(BF16).reshape(shape3)
            for kt in range(n_tiles):
                blk = vt[g * HEAD_DIM:(g + 1) * HEAD_DIM, kt * KEY_TILE:(kt + 1) * KEY_TILE]
                v_o[g, kt, 0:HEAD_DIM, 0:KEY_TILE] = blk
                v_o[g, kt, 0:HEAD_DIM, KEY_TILE:] = zeros
                v_o[g, kt, HEAD_DIM:, 0:KEY_TILE] = zeros
                v_o[g, kt, HEAD_DIM:, KEY_TILE:] = blk

    emit(s_ref, kn_ref[1:2, :], ks_o, vs_o)
    emit(w_ref, kn_ref[2:3, :], kw_o, vw_o)


def _prep(proj, pek, w1k, w2k, pev, w1v, w2v, kn, layer, B, S):
    n_tiles = S // KEY_TILE
    kv_blk = COL_KV // 256
    full = lambda a: _layer_spec(a.shape[1:], layer, lambda b: (0,) * (a.ndim - 1))
    kc_shape = jax.ShapeDtypeStruct((B, KV_GROUPS, 2 * N_CMP_PAD, LANES), BF16)
    vc_shape = jax.ShapeDtypeStruct((B, KV_GROUPS, LANES, 2 * N_CMP_PAD), BF16)
    k_shape = jax.ShapeDtypeStruct((B, KV_GROUPS, n_tiles, 2 * KEY_TILE, LANES), BF16)
    v_shape = jax.ShapeDtypeStruct((B, KV_GROUPS, n_tiles, LANES, 2 * KEY_TILE), BF16)
    kc_spec = pl.BlockSpec((None, KV_GROUPS, 2 * N_CMP_PAD, LANES), lambda b: (b, 0, 0, 0))
    vc_spec = pl.BlockSpec((None, KV_GROUPS, LANES, 2 * N_CMP_PAD), lambda b: (b, 0, 0, 0))
    k_spec = pl.BlockSpec((None, KV_GROUPS, n_tiles, 2 * KEY_TILE, LANES), lambda b: (b, 0, 0, 0, 0))
    v_spec = pl.BlockSpec((None, KV_GROUPS, n_tiles, LANES, 2 * KEY_TILE), lambda b: (b, 0, 0, 0, 0))
    return pl.pallas_call(
        _prep_kernel,
        grid=(B,),
        in_specs=[
            pl.BlockSpec((S, LANES), lambda b: (b, COL_KV // LANES)),
            pl.BlockSpec((S, LANES), lambda b: (b, COL_KV // LANES + 1)),
            pl.BlockSpec((S, 256), lambda b: (b, kv_blk + 1)),
            pl.BlockSpec((S, 256), lambda b: (b, kv_blk + 2)),
            full(pek), full(w1k), full(w2k), full(pev), full(w1v), full(w2v), full(kn),
        ],
        out_specs=[kc_spec, vc_spec, k_spec, v_spec, k_spec, v_spec],
        out_shape=[kc_shape, vc_shape, k_shape, v_shape, k_shape, v_shape],
        compiler_params=_cparams(("parallel",)),
        name="kv_prep",
    )(proj, proj, proj, proj, pek, w1k, w2k, pev, w1v, w2v, kn)


IMP_PAD = SUBLANES
SEL_CHUNK = 4


def _nsa_kernel(q_ref, gate_ref, qn_ref, kc_ref, vc_ref, ks_ref, vs_ref, kw_ref, vw_ref,
                eneg_ref, o_ref,
                qp_ref, pc_ref, pt_ref, nsel_ref, s_ref, bias_ref, p_ref, acc_ref, st_ref):
    i = pl.program_id(2)
    lane = lax.broadcasted_iota(jnp.int32, (Q_BLOCK, LANES), 1)
    row = lax.broadcasted_iota(jnp.int32, (Q_BLOCK, LANES), 0)
    lo = lane < HEAD_DIM
    t_q = i * Q_BLOCK + lane
    cols_all = HEAD_PAIRS * Q_BLOCK

    def tile_cols(a):
        return jnp.concatenate([a] * HEAD_PAIRS, axis=1)

    scale = HEAD_DIM ** -0.5 * LOG2E
    for c in range(HEAD_PAIRS):
        x = q_ref[:, c * LANES:(c + 1) * LANES]
        qp_ref[c * Q_BLOCK:(c + 1) * Q_BLOCK, :] = (_half_rms(x, lo, qn_ref[...]) * scale).astype(BF16)

    sc = _dot_nt(kc_ref[...], qp_ref[...])
    valid_c = row * CMP_STRIDE + (CMP_BLOCK - 1) <= t_q
    psum = jnp.zeros((N_CMP_PAD, Q_BLOCK), F32)
    for c in range(HEAD_PAIRS):
        for hf in range(2):
            s = sc[hf * N_CMP_PAD:(hf + 1) * N_CMP_PAD, c * Q_BLOCK:(c + 1) * Q_BLOCK]
            s = jnp.where(valid_c, s, MASK_VALUE)
            e = jnp.where(valid_c, jnp.exp2(s - jnp.max(s, axis=0, keepdims=True)), 0.0)
            den = jnp.sum(e, axis=0, keepdims=True)
            p = e / jnp.where(den > 0.0, den, 1.0)
            psum = psum + p
            pc_ref[hf * N_CMP_PAD:(hf + 1) * N_CMP_PAD, c * Q_BLOCK:(c + 1) * Q_BLOCK] = p.astype(BF16)
    o_cmp = _dot(vc_ref[...], pc_ref[...])

    n_blk = N_CMP_PAD // CMP_PER_SEL
    pt_ref[0:IMP_PAD, :] = jnp.zeros((IMP_PAD, LANES), F32)
    pt_ref[IMP_PAD:, :] = psum
    imp = pt_ref[pl.ds(IMP_PAD - 1, n_blk, stride=CMP_PER_SEL), :]
    for d in range(CMP_PER_SEL):
        imp = imp + pt_ref[pl.ds(IMP_PAD + d, n_blk, stride=CMP_PER_SEL), :]
    blk = lax.broadcasted_iota(jnp.int32, (n_blk, LANES), 0)
    cur = (i * Q_BLOCK + lax.broadcasted_iota(jnp.int32, (n_blk, LANES), 1)) // SEL_BLOCK
    forced = (blk == 0) | (blk == cur) | (blk == cur - 1)
    score = jnp.where(blk > cur, -1.0, jnp.where(forced, FORCE_SCORE, imp))
    rank = jnp.zeros((n_blk, LANES), F32)
    for j in range(n_blk):
        other = jnp.broadcast_to(score[j:j + 1, :], (n_blk, LANES))
        beats = (other > score) | ((other == score) & (blk > j))
        rank = rank + jnp.where(beats, 1.0, 0.0)
    not_sel = jnp.where(rank < float(N_SEL), 0.0, 1.0)
    not_sel = jnp.concatenate([not_sel, jnp.zeros((LANES - n_blk, LANES), F32)], axis=0)
    nsel_ref[...] = not_sel.astype(BF16)

    rows2 = 2 * KEY_TILE

    def per_head_half(vals, even, odd):
        return jnp.concatenate([vals[:HEAD_DIM] * even, vals[HEAD_DIM:] * odd], axis=0)

    M_E, M_O, L_E, L_O, A_E, A_O = (SUBLANES * r for r in range(6))
    st_row = lambda r: st_ref[r:r + 1, :]
    last_chunk = i // SEL_CHUNK

    def chunk_scores_and_bias(r):
        for c in range(SEL_CHUNK):
            kt = r * SEL_CHUNK + c
            s_ref[c * rows2:(c + 1) * rows2, :] = _dot_nt(ks_ref[kt], qp_ref[...])
            bias = _dot(eneg_ref[kt], nsel_ref[...])
            bias_ref[c * KEY_TILE:(c + 1) * KEY_TILE, :] = jnp.where(kt * KEY_TILE + row <= t_q, bias, MASK_VALUE)

    def chunk_pv(r):
        pv = None
        for c in range(SEL_CHUNK):
            d = _dot(vs_ref[r * SEL_CHUNK + c], p_ref[c * rows2:(c + 1) * rows2, :])
            pv = d if pv is None else pv + d
        return per_head_half(acc_ref[...], st_row(A_E), st_row(A_O)) + pv

    chunk_scores_and_bias(0)
    acc_ref[...] = jnp.zeros(acc_ref.shape, F32)
    p_ref[...] = jnp.zeros(p_ref.shape, BF16)
    st_ref[M_E:L_E, :] = jnp.full((2 * SUBLANES, cols_all), MASK_VALUE, F32)
    st_ref[L_E:, :] = jnp.zeros((4 * SUBLANES, cols_all), F32)

    def sel_body(r, carry):
        acc_new = chunk_pv(jnp.maximum(r - 1, 0))
        p_new = [[None, None] for _ in range(SEL_CHUNK)]
        for hf, (m_r, l_r, a_r) in enumerate(((M_E, L_E, A_E), (M_O, L_O, A_O))):
            s = [s_ref[c * rows2 + hf * KEY_TILE:c * rows2 + (hf + 1) * KEY_TILE, :]
                 + tile_cols(bias_ref[c * KEY_TILE:(c + 1) * KEY_TILE, :]) for c in range(SEL_CHUNK)]
            m_prev = st_row(m_r)
            m_new = m_prev
            for c in range(SEL_CHUNK):
                m_new = jnp.maximum(m_new, jnp.max(s[c], axis=0, keepdims=True))
            alpha = jnp.exp2(m_prev - m_new)
            l_new = alpha * st_row(l_r)
            for c in range(SEL_CHUNK):
                p = jnp.exp2(s[c] - m_new)
                l_new = l_new + jnp.sum(p, axis=0, keepdims=True)
                p_new[c][hf] = p.astype(BF16)
            st_ref[l_r:l_r + 1, :] = l_new
            st_ref[m_r:m_r + 1, :] = m_new
            st_ref[a_r:a_r + 1, :] = alpha
        acc_ref[...] = acc_new
        for c in range(SEL_CHUNK):
            p_ref[c * rows2:(c + 1) * rows2, :] = jnp.concatenate(p_new[c], axis=0)
        chunk_scores_and_bias(jnp.minimum(r + 1, last_chunk))
        return carry

    lax.fori_loop(0, last_chunk + 1, sel_body, 0)
    o_sel = per_head_half(chunk_pv(last_chunk), 1.0 / st_row(L_E), 1.0 / st_row(L_O))

    n_win = WINDOW // KEY_TILE + 1
    s_win, v_idx = [], []
    for c in range(n_win):
        kt = i - (n_win - 1) + c
        idx = jnp.maximum(kt, 0)
        key = kt * KEY_TILE + row
        lag = t_q - key
        bias = jnp.where((key >= 0) & (lag >= 0) & (lag < WINDOW), 0.0, MASK_VALUE)
        s_win.append(_dot_nt(kw_ref[idx], qp_ref[...]) + jnp.concatenate([tile_cols(bias)] * 2, axis=0))
        v_idx.append(idx)
    l_w, p_w = [], [[None, None] for _ in range(n_win)]
    for hf in range(2):
        half = slice(hf * KEY_TILE, (hf + 1) * KEY_TILE)
        m = jnp.max(s_win[0][half], axis=0, keepdims=True)
        for c in range(1, n_win):
            m = jnp.maximum(m, jnp.max(s_win[c][half], axis=0, keepdims=True))
        l = jnp.zeros((1, cols_all), F32)
        for c in range(n_win):
            p = jnp.exp2(s_win[c][half] - m)
            l = l + jnp.sum(p, axis=0, keepdims=True)
            p_w[c][hf] = p.astype(BF16)
        l_w.append(l)
    pv = None
    for c in range(n_win):
        d = _dot(vw_ref[v_idx[c]], jnp.concatenate(p_w[c], axis=0))
        pv = d if pv is None else pv + d
    o_win = per_head_half(pv, 1.0 / l_w[0], 1.0 / l_w[1])

    gates = jax.nn.sigmoid(gate_ref[...]).T

    def gate_pair(br, c):
        r = br * HEADS_PER_GROUP + 2 * c
        return jnp.concatenate([jnp.broadcast_to(gates[r:r + 1, :], (HEAD_DIM, Q_BLOCK)),
                                jnp.broadcast_to(gates[r + 1:r + 2, :], (HEAD_DIM, Q_BLOCK))], axis=0)

    for c in range(HEAD_PAIRS):
        cols = slice(c * Q_BLOCK, (c + 1) * Q_BLOCK)
        o_t = (gate_pair(0, c) * o_cmp[:, cols] + gate_pair(1, c) * o_sel[:, cols]
               + gate_pair(2, c) * o_win[:, cols])
        o_ref[:, c * LANES:(c + 1) * LANES] = o_t.T


def _nsa(proj, qn, kc, vc, ks, vs, kw, vw, eneg, layer, B, S):
    n_q = S // Q_BLOCK
    n_tiles = S // KEY_TILE
    gate_blk = COL_GATE // LANES
    kc_spec = pl.BlockSpec((None, None, 2 * N_CMP_PAD, LANES), lambda b, g, i: (b, g, 0, 0))
    vc_spec = pl.BlockSpec((None, None, LANES, 2 * N_CMP_PAD), lambda b, g, i: (b, g, 0, 0))
    k_spec = pl.BlockSpec((None, None, n_tiles, 2 * KEY_TILE, LANES), lambda b, g, i: (b, g, 0, 0, 0))
    v_spec = pl.BlockSpec((None, None, n_tiles, LANES, 2 * KEY_TILE), lambda b, g, i: (b, g, 0, 0, 0))
    cols_all = HEAD_PAIRS * Q_BLOCK
    return pl.pallas_call(
        _nsa_kernel,
        grid=(B, KV_GROUPS, n_q),
        in_specs=[
            pl.BlockSpec((Q_BLOCK, GROUP_WIDTH), lambda b, g, i: (b * n_q + i, g)),
            pl.BlockSpec((Q_BLOCK, LANES), lambda b, g, i: (b * n_q + i, gate_blk + g)),
            _layer_spec((1, LANES), layer, lambda b, g, i: (0, 0)),
            kc_spec, vc_spec, k_spec, v_spec, k_spec, v_spec,
            pl.BlockSpec(eneg.shape, lambda b, g, i: (0, 0, 0)),
        ],
        out_specs=pl.BlockSpec((Q_BLOCK, GROUP_WIDTH), lambda b, g, i: (b * n_q + i, g)),
        out_shape=jax.ShapeDtypeStruct((B * S, NSA_WIDTH), F32),
        scratch_shapes=[
            pltpu.VMEM((cols_all, LANES), BF16),
            pltpu.VMEM((2 * N_CMP_PAD, cols_all), BF16),
            pltpu.VMEM((IMP_PAD + N_CMP_PAD, LANES), F32),
            pltpu.VMEM((LANES, Q_BLOCK), BF16),
            pltpu.VMEM((SEL_CHUNK * 2 * KEY_TILE, cols_all), F32),
            pltpu.VMEM((SEL_CHUNK * KEY_TILE, Q_BLOCK), F32),
            pltpu.VMEM((SEL_CHUNK * 2 * KEY_TILE, cols_all), BF16),
            pltpu.VMEM((LANES, cols_all), F32),
            pltpu.VMEM((6 * SUBLANES, cols_all), F32),
        ],
        compiler_params=_cparams(("parallel", "parallel", "arbitrary")),
        name="nsa_attention",
    )(proj, proj, qn, kc, vc, ks, vs, kw, vw, eneg)


def _conv_kernel(b_ref, c_ref, x_ref, w_ref, o_ref):
    u = c_ref[...] * x_ref[...]
    row = lax.broadcasted_iota(jnp.int32, u.shape, 0)
    u1 = jnp.where(row >= 1, pltpu.roll(u, 1, 0), 0.0)
    u2 = jnp.where(row >= 2, pltpu.roll(u, 2, 0), 0.0)
    y = w_ref[2:3, :] * u + w_ref[1:2, :] * u1 + w_ref[0:1, :] * u2
    o_ref[...] = b_ref[...] * y


def _conv(proj, conv_w, layer, B, S, cw=256):
    nb = CONV_WIDTH // cw
    base = COL_CONV // cw
    return pl.pallas_call(
        _conv_kernel,
        grid=(B, nb),
        in_specs=[
            pl.BlockSpec((S, cw), lambda b, j: (b, base + j)),
            pl.BlockSpec((S, cw), lambda b, j: (b, base + nb + j)),
            pl.BlockSpec((S, cw), lambda b, j: (b, base + 2 * nb + j)),
            _layer_spec((CONV_K, cw), layer, lambda b, j: (0, j)),
        ],
        out_specs=pl.BlockSpec((S, cw), lambda b, j: (b, j)),
        out_shape=jax.ShapeDtypeStruct((B * S, CONV_WIDTH), F32),
        compiler_params=_cparams(("parallel", "parallel")),
        name="gated_conv",
    )(proj, proj, proj, conv_w)


def _mix_out_kernel(x_ref, a_ref, b_ref, ga_ref, gb_ref, wa_ref, wb_ref, o_ref):
    a = _rms(a_ref[...], ga_ref[...]).astype(BF16)
    b = _rms(b_ref[...], gb_ref[...]).astype(BF16)
    o_ref[...] = x_ref[...] + _dot(a, wa_ref[...]) + _dot(b, wb_ref[...])


def _mix_out(x, oa, ob, ga, gb, w_out, layer, tm=256):
    T = x.shape[0]
    const = lambda shape, idx: _layer_spec(shape, layer, lambda i: idx)
    return pl.pallas_call(
        _mix_out_kernel,
        grid=(T // tm,),
        in_specs=[
            pl.BlockSpec((tm, D_MODEL), lambda i: (i, 0)),
            pl.BlockSpec((tm, NSA_WIDTH), lambda i: (i, 0)),
            pl.BlockSpec((tm, CONV_WIDTH), lambda i: (i, 0)),
            const((1, NSA_WIDTH), (0, 0)), const((1, CONV_WIDTH), (0, 0)),
            const((NSA_WIDTH, D_MODEL), (0, 0)), const((CONV_WIDTH, D_MODEL), (1, 0)),
        ],
        out_specs=pl.BlockSpec((tm, D_MODEL), lambda i: (i, 0)),
        out_shape=jax.ShapeDtypeStruct((T, D_MODEL), F32),
        compiler_params=_cparams(("parallel",)),
        name="mixer_out",
    )(x, oa, ob, ga, gb, w_out, w_out)


def _xattn_kernel(x_ref, g_ref, wq_ref, qn_ref, kn_ref, kv_ref, wo_ref, o_ref, oh_ref):
    x = x_ref[...]
    h = _rms(x, g_ref[...]).astype(BF16)
    q = _dot(h, wq_ref[...])
    scale = XA_HEAD_DIM ** -0.5
    for hd in range(XA_HEADS):
        cols = slice(hd * XA_HEAD_DIM, (hd + 1) * XA_HEAD_DIM)
        qh = _rms(q[:, cols], qn_ref[...]).astype(BF16)
        kh = _rms(kv_ref[:, cols], kn_ref[...]).astype(BF16)
        vh = kv_ref[:, XA_WIDTH + hd * XA_HEAD_DIM:XA_WIDTH + (hd + 1) * XA_HEAD_DIM].astype(BF16)
        s = _dot_nt(qh, kh) * scale
        e = jnp.exp(s - jnp.max(s, axis=-1, keepdims=True))
        p = (e / jnp.sum(e, axis=-1, keepdims=True)).astype(BF16)
        oh_ref[:, cols] = _dot(p, vh).astype(BF16)
    o_ref[...] = x + _dot(oh_ref[...], wo_ref[...])


def _xattn(x, g, wq, qn, kn, kv, wo, layer, B, S, tm=256):
    M = kv.shape[0] // B
    n_t = S // tm
    const = lambda shape: _layer_spec(shape, layer, lambda b, i: (0, 0))
    return pl.pallas_call(
        _xattn_kernel,
        grid=(B, n_t),
        in_specs=[
            pl.BlockSpec((tm, D_MODEL), lambda b, i: (b * n_t + i, 0)),
            const((1, D_MODEL)), const((D_MODEL, XA_WIDTH)),
            const((1, XA_HEAD_DIM)), const((1, XA_HEAD_DIM)),
            pl.BlockSpec((M, 2 * XA_WIDTH), lambda b, i: (b, 0)),
            const((XA_WIDTH, D_MODEL)),
        ],
        out_specs=pl.BlockSpec((tm, D_MODEL), lambda b, i: (b * n_t + i, 0)),
        out_shape=jax.ShapeDtypeStruct((B * S, D_MODEL), F32),
        scratch_shapes=[pltpu.VMEM((tm, XA_WIDTH), BF16)],
        compiler_params=_cparams(("parallel", "parallel")),
        name="mem_xattn",
    )(x, g, wq, qn, kn, kv, wo)


def _dup(v):
    return jnp.concatenate([v, v], axis=-1)


def _reorder_w_in(w_in):
    L = w_in.shape[0]
    n_gate = N_BRANCH * NSA_HEADS
    kv_end = NSA_WIDTH + 6 * LANES
    w_in = w_in.astype(BF16)
    gate = w_in[:, :, kv_end:kv_end + n_gate]
    gate = gate.reshape(L, D_MODEL, N_BRANCH, KV_GROUPS, HEADS_PER_GROUP).transpose(0, 1, 3, 2, 4)
    gate = gate.reshape(L, D_MODEL, KV_GROUPS, N_BRANCH * HEADS_PER_GROUP)
    gate = jnp.pad(gate, ((0, 0), (0, 0), (0, 0), (0, LANES - N_BRANCH * HEADS_PER_GROUP)))
    gate = gate.reshape(L, D_MODEL, KV_GROUPS * LANES)
    return jnp.concatenate([w_in[:, :, :kv_end], gate, w_in[:, :, kv_end + n_gate:]], axis=-1)


def _selection_mask_tiles(S):
    j = jnp.arange(LANES)[None, None, :]
    k = (jnp.arange(S // KEY_TILE)[:, None, None] * KEY_TILE + jnp.arange(KEY_TILE)[None, :, None])
    return jnp.where(k // SEL_BLOCK == j, MASK_VALUE, 0.0).astype(BF16)


def kernel(x, mem, ffn1_norm, ffn1_w_gate, ffn1_w_up, ffn1_w_down, mix_norm, w_in, cmp_pe_k, cmp_w1_k, cmp_w2_k, cmp_pe_v, cmp_w1_v, cmp_w2_v, q_norm, k_norm, conv_w, out_norm_nsa, out_norm_conv, w_out, xattn_norm, mem_norm, xattn_w_q, xattn_w_kv, xattn_q_norm, xattn_k_norm, xattn_w_o, ffn2_norm, ffn2_w_gate, ffn2_w_up, ffn2_w_down):
    B, S, D = x.shape
    L = w_in.shape[0]
    T = B * S
    M = mem.shape[1]
    bf = lambda w: w.astype(BF16)
    row = lambda a: a.reshape(L, 1, a.shape[-1])

    f1g, f1u, f1d = bf(ffn1_w_gate), bf(ffn1_w_up), bf(ffn1_w_down)
    f2g, f2u, f2d = bf(ffn2_w_gate), bf(ffn2_w_up), bf(ffn2_w_down)
    w_in_r = _reorder_w_in(w_in)
    w_out_b = bf(w_out)
    wq_b, wkv_b, wo_b = bf(xattn_w_q), bf(xattn_w_kv), bf(xattn_w_o)
    w1k = bf(_dup(cmp_w1_k.reshape(L, CMP_BLOCK, HEAD_DIM, CMP_HIDDEN).swapaxes(2, 3)).swapaxes(2, 3))
    w1v = bf(_dup(cmp_w1_v.reshape(L, CMP_BLOCK, HEAD_DIM, CMP_HIDDEN).swapaxes(2, 3)).swapaxes(2, 3))
    w2k, w2v = bf(_dup(cmp_w2_k)), bf(_dup(cmp_w2_v))
    pek, pev = _dup(cmp_pe_k), _dup(cmp_pe_v)
    qn, kn = row(_dup(q_norm)), _dup(k_norm)
    eneg = _selection_mask_tiles(S)

    f1n, f2n, mixn = row(ffn1_norm), row(ffn2_norm), row(mix_norm)
    ona, onc = row(out_norm_nsa), row(out_norm_conv)
    xan, memn = row(xattn_norm), row(mem_norm)
    xqn, xkn = row(xattn_q_norm), row(xattn_k_norm)

    xs = x.reshape(T, D)
    mem2 = mem.reshape(B * M, D)
    for l in range(L):
        xs = _ffn(xs, f1n, f1g, f1u, f1d, l)
        proj = _norm_matmul(xs, mixn, w_in_r, l, 1024, 1024, "mixer_in")
        kc, vc, ks, vs, kw, vw = _prep(proj, pek, w1k, w2k, pev, w1v, w2v, kn, l, B, S)
        o_a = _nsa(proj, qn, kc, vc, ks, vs, kw, vw, eneg, l, B, S)
        o_b = _conv(proj, conv_w, l, B, S)
        xs = _mix_out(xs, o_a, o_b, ona, onc, w_out_b, l)
        kv = _norm_matmul(mem2, memn, wkv_b, l, 512, 1024, "mem_kv")
        xs = _xattn(xs, xan, wq_b, xqn, xkn, kv, wo_b, l, B, S)
        xs = _ffn(xs, f2n, f2g, f2u, f2d, l)
    return xs.reshape(B, S, D)
```

```python
import math

import jax
import jax.numpy as jnp
from jax import lax
from jax.experimental import pallas as pl
from jax.experimental.pallas import tpu as pltpu

F32 = jnp.float32
BF16 = jnp.bfloat16

D_MODEL = 2048
D_FF = 5632
EPS = 1e-6
MASK_VALUE = -1e30
FORCE_SCORE = 1e4
LOG2E = math.log2(math.e)

NSA_HEADS = 16
KV_GROUPS = 2
HEADS_PER_GROUP = NSA_HEADS // KV_GROUPS
HEAD_DIM = 64
NSA_WIDTH = NSA_HEADS * HEAD_DIM
GROUP_WIDTH = NSA_WIDTH // KV_GROUPS
HEAD_PAIRS = GROUP_WIDTH // 128
N_BRANCH = 3
CMP_BLOCK = 32
CMP_STRIDE = 16
CMP_HIDDEN = 4 * HEAD_DIM
SEL_BLOCK = 64
N_SEL = 8
WINDOW = 512
Q_BLOCK = 128
CONV_WIDTH = 1024
CONV_K = 3
XA_HEADS = 4
XA_HEAD_DIM = 128
XA_WIDTH = XA_HEADS * XA_HEAD_DIM

LANES = 128
SUBLANES = 8
KEY_TILE = 128
N_CMP_PAD = 128
CMP_PER_SEL = SEL_BLOCK // CMP_STRIDE
V_ROWS = LANES + 16
L_ROW = LANES

COL_Q = 0
COL_KV = NSA_WIDTH
COL_GATE = COL_KV + 6 * LANES
COL_CONV = COL_GATE + KV_GROUPS * LANES
PROJ_COLS = COL_CONV + 3 * CONV_WIDTH

VMEM_LIMIT = 56 * 1024 * 1024


def _cparams(sem):
    return pltpu.CompilerParams(dimension_semantics=sem, vmem_limit_bytes=VMEM_LIMIT)


def _rms(x, g):
    ms = jnp.mean(x * x, axis=-1, keepdims=True)
    return x * lax.rsqrt(ms + EPS) * g


def _dot(a, b):
    return jnp.dot(a, b, preferred_element_type=F32)


def _dot_nt(a, b):
    return lax.dot_general(a, b, (((1,), (1,)), ((), ())), preferred_element_type=F32)


def _layer_spec(tail_shape, layer, tail_index):
    return pl.BlockSpec((None,) + tuple(tail_shape), lambda *g: (layer,) + tuple(tail_index(*g)))


def _cast_kernel(w_ref, o_ref):
    o_ref[...] = w_ref[...].astype(BF16)


def _cast_bf16(w, rows):
    L, R, C = w.shape
    spec = pl.BlockSpec((None, rows, C), lambda l, r: (l, r, 0))
    return pl.pallas_call(
        _cast_kernel,
        grid=(L, R // rows),
        in_specs=[spec],
        out_specs=spec,
        out_shape=jax.ShapeDtypeStruct(w.shape, BF16),
        compiler_params=_cparams(("parallel", "parallel")),
        name="cast_bf16",
    )(w)


FFN_SUB = 256


def _ffn_kernel(x_ref, g_ref, wg_ref, wu_ref, wd_ref, o_ref, h_ref):
    j = pl.program_id(1)

    @pl.when(j == 0)
    def _():
        h_ref[...] = _rms(x_ref[...], g_ref[...]).astype(BF16)
        o_ref[...] = jnp.zeros(o_ref.shape, F32)

    h = h_ref[...]
    part = None
    for c in range(wg_ref.shape[1] // FFN_SUB):
        cols = slice(c * FFN_SUB, (c + 1) * FFN_SUB)
        a = _dot(h, wg_ref[:, cols])
        u = _dot(h, wu_ref[:, cols])
        act = (a * jax.nn.sigmoid(a) * u).astype(BF16)
        d = _dot(act, wd_ref[cols, :])
        part = d if part is None else part + d
    o_ref[...] += part

    @pl.when(j == pl.num_programs(1) - 1)
    def _():
        o_ref[...] = x_ref[...] + 0.5 * o_ref[...]


def _ffn(x, g, wg, wu, wd, layer, tm=512, tf=512):
    T = x.shape[0]
    return pl.pallas_call(
        _ffn_kernel,
        grid=(T // tm, D_FF // tf),
        in_specs=[
            pl.BlockSpec((tm, D_MODEL), lambda i, j: (i, 0)),
            _layer_spec((1, D_MODEL), layer, lambda i, j: (0, 0)),
            _layer_spec((D_MODEL, tf), layer, lambda i, j: (0, j)),
            _layer_spec((D_MODEL, tf), layer, lambda i, j: (0, j)),
            _layer_spec((tf, D_MODEL), layer, lambda i, j: (j, 0)),
        ],
        out_specs=pl.BlockSpec((tm, D_MODEL), lambda i, j: (i, 0)),
        out_shape=jax.ShapeDtypeStruct((T, D_MODEL), F32),
        scratch_shapes=[pltpu.VMEM((tm, D_MODEL), BF16)],
        compiler_params=_cparams(("parallel", "arbitrary")),
        name="ffn",
    )(x, g, wg, wu, wd)


def _norm_matmul_kernel(x_ref, g_ref, w_ref, o_ref, h_ref):
    @pl.when(pl.program_id(1) == 0)
    def _():
        h_ref[...] = _rms(x_ref[...], g_ref[...]).astype(BF16)

    o_ref[...] = _dot(h_ref[...], w_ref[...])


def _norm_matmul(x, g, w, layer, tm, tn, name):
    T, K = x.shape
    N = w.shape[2]
    return pl.pallas_call(
        _norm_matmul_kernel,
        grid=(T // tm, N // tn),
        in_specs=[
            pl.BlockSpec((tm, K), lambda i, j: (i, 0)),
            _layer_spec((1, K), layer, lambda i, j: (0, 0)),
            _layer_spec((K, tn), layer, lambda i, j: (0, j)),
        ],
        out_specs=pl.BlockSpec((tm, tn), lambda i, j: (i, j)),
        out_shape=jax.ShapeDtypeStruct((T, N), F32),
        scratch_shapes=[pltpu.VMEM((tm, K), BF16)],
        compiler_params=_cparams(("parallel", "arbitrary")),
        name=name,
    )(x, g, w)


def _half_rms(x, lo, gain):
    sq = x * x
    s_lo = jnp.sum(jnp.where(lo, sq, 0.0), axis=-1, keepdims=True)
    s_hi = jnp.sum(jnp.where(lo, 0.0, sq), axis=-1, keepdims=True)
    inv = jnp.where(lo, lax.rsqrt(s_lo * (1.0 / HEAD_DIM) + EPS),
                    lax.rsqrt(s_hi * (1.0 / HEAD_DIM) + EPS))
    return x * inv * gain


def _prep_kernel(ck_ref, cv_ref, s_ref, w_ref, pek_ref, w1k_ref, w2k_ref, pev_ref, w1v_ref, w2v_ref,
                 kn_ref, eneg_ref, kc_o, vc_o, ks_o, vs_o, kw_o, vw_o):
    S = cv_ref.shape[0]
    n_tiles = S // KEY_TILE
    lo = lax.broadcasted_iota(jnp.int32, (1, LANES), 1) < HEAD_DIM
    half = CMP_BLOCK // 2

    def compress(src_ref, pe_ref, w1_ref, w2_ref):
        acc_a = jnp.zeros((2 * N_CMP_PAD, CMP_HIDDEN), F32)
        acc_b = jnp.zeros((2 * N_CMP_PAD, CMP_HIDDEN), F32)
        for l in range(half):
            x = src_ref[pl.ds(l, N_CMP_PAD, stride=CMP_STRIDE), :]
            for acc_is_b, ll in ((False, l), (True, l + half)):
                xp = x + pe_ref[ll:ll + 1, :]
                x2 = jnp.concatenate([jnp.where(lo, xp, 0.0), jnp.where(lo, 0.0, xp)],
                                     axis=0).astype(BF16)
                d = _dot(x2, w1_ref[ll])
                if acc_is_b:
                    acc_b = acc_b + d
                else:
                    acc_a = acc_a + d
        hidden = acc_a + pltpu.roll(acc_b, 2 * N_CMP_PAD - 1, 0)
        act = jax.nn.gelu(hidden, approximate=True).astype(BF16)
        return _dot(act, w2_ref[...])

    kc = _half_rms(compress(ck_ref, pek_ref, w1k_ref, w2k_ref), lo, kn_ref[0:1, :])
    vc = compress(cv_ref, pev_ref, w1v_ref, w2v_ref)
    top = lax.broadcasted_iota(jnp.int32, (LANES, 1), 0) < HEAD_DIM
    for g in range(KV_GROUPS):
        rows = slice(g * N_CMP_PAD, (g + 1) * N_CMP_PAD)
        kc_o[g, 0:N_CMP_PAD, :] = jnp.where(lo, kc[rows], 0.0).astype(BF16)
        kc_o[g, N_CMP_PAD:, :] = jnp.where(lo, 0.0, kc[rows]).astype(BF16)
        vt = vc[rows].T
        vc_o[g, :, 0:N_CMP_PAD] = jnp.where(top, vt, 0.0).astype(BF16)
        vc_o[g, :, N_CMP_PAD:] = jnp.where(top, 0.0, vt).astype(BF16)

    tail_row = lax.broadcasted_iota(jnp.int32, (V_ROWS - LANES, 2 * KEY_TILE), 0)
    tail_col = lax.broadcasted_iota(jnp.int32, (V_ROWS - LANES, 2 * KEY_TILE), 1)
    v_tail = jnp.where(tail_row == tail_col // KEY_TILE, 1.0, 0.0).astype(BF16)

    def emit(src_ref, gain, k_o, v_o, with_mask):
        k = _half_rms(src_ref[:, 0:LANES], lo, gain)
        k_sw = pltpu.roll(k, HEAD_DIM, 1)
        vt = src_ref[:, LANES:2 * LANES].T.astype(BF16)
        zeros = jnp.zeros((HEAD_DIM, KEY_TILE), BF16)
        for g in range(KV_GROUPS):
            k_lo, k_hi = (k, k_sw) if g == 0 else (k_sw, k)
            shape3 = (n_tiles, KEY_TILE, LANES)
            k_o[g, :, 0:KEY_TILE, 0:LANES] = jnp.where(lo, k_lo, 0.0).astype(BF16).reshape(shape3)
            k_o[g, :, KEY_TILE:, 0:LANES] = jnp.where(lo, 0.0, k_hi).astype(BF16).reshape(shape3)
            if with_mask:
                k_o[g, :, 0:KEY_TILE, LANES:] = eneg_ref[...]
                k_o[g, :, KEY_TILE:, LANES:] = eneg_ref[...]
            for kt in range(n_tiles):
                blk = vt[g * HEAD_DIM:(g + 1) * HEAD_DIM, kt * KEY_TILE:(kt + 1) * KEY_TILE]
                v_o[g, kt, 0:HEAD_DIM, 0:KEY_TILE] = blk
                v_o[g, kt, 0:HEAD_DIM, KEY_TILE:] = zeros
                v_o[g, kt, HEAD_DIM:LANES, 0:KEY_TILE] = zeros
                v_o[g, kt, HEAD_DIM:LANES, KEY_TILE:] = blk
                v_o[g, kt, LANES:, :] = v_tail

    emit(s_ref, kn_ref[1:2, :], ks_o, vs_o, True)
    emit(w_ref, kn_ref[2:3, :], kw_o, vw_o, False)


def _prep(proj, pek, w1k, w2k, pev, w1v, w2v, kn, eneg, layer, B, S):
    n_tiles = S // KEY_TILE
    kv_blk = COL_KV // 256
    full = lambda a: _layer_spec(a.shape[1:], layer, lambda b: (0,) * (a.ndim - 1))
    per_batch = lambda *tail: (jax.ShapeDtypeStruct((B, KV_GROUPS) + tail, BF16),
                               pl.BlockSpec((None, KV_GROUPS) + tail, lambda b: (b,) + (0,) * (len(tail) + 1)))
    kc_shape, kc_spec = per_batch(2 * N_CMP_PAD, LANES)
    vc_shape, vc_spec = per_batch(LANES, 2 * N_CMP_PAD)
    ks_shape, ks_spec = per_batch(n_tiles, 2 * KEY_TILE, 2 * LANES)
    kw_shape, kw_spec = per_batch(n_tiles, 2 * KEY_TILE, LANES)
    v_shape, v_spec = per_batch(n_tiles, V_ROWS, 2 * KEY_TILE)
    return pl.pallas_call(
        _prep_kernel,
        grid=(B,),
        in_specs=[
            pl.BlockSpec((S, LANES), lambda b: (b, COL_KV // LANES)),
            pl.BlockSpec((S, LANES), lambda b: (b, COL_KV // LANES + 1)),
            pl.BlockSpec((S, 256), lambda b: (b, kv_blk + 1)),
            pl.BlockSpec((S, 256), lambda b: (b, kv_blk + 2)),
            full(pek), full(w1k), full(w2k), full(pev), full(w1v), full(w2v), full(kn),
            pl.BlockSpec(eneg.shape, lambda b: (0, 0, 0)),
        ],
        out_specs=[kc_spec, vc_spec, ks_spec, v_spec, kw_spec, v_spec],
        out_shape=[kc_shape, vc_shape, ks_shape, v_shape, kw_shape, v_shape],
        compiler_params=_cparams(("parallel",)),
        name="kv_prep",
    )(proj, proj, proj, proj, pek, w1k, w2k, pev, w1v, w2v, kn, eneg)


IMP_PAD = SUBLANES
SEL_CHUNK = 4


def _nsa_kernel(q_ref, gate_ref, qn_ref, kc_ref, vc_ref, ks_ref, vs_ref, kw_ref, vw_ref,
                o_ref, qs_ref, pc_ref, pt_ref, gt_ref, og_ref, s_ref, p_ref, acc_ref, st_ref):
    i = pl.program_id(2)
    lane = lax.broadcasted_iota(jnp.int32, (Q_BLOCK, LANES), 1)
    row = lax.broadcasted_iota(jnp.int32, (Q_BLOCK, LANES), 0)
    lo = lane < HEAD_DIM
    t_q = i * Q_BLOCK + lane
    cols_all = HEAD_PAIRS * Q_BLOCK

    def tile_cols(a):
        return jnp.concatenate([a] * HEAD_PAIRS, axis=1)

    scale = HEAD_DIM ** -0.5 * LOG2E
    for c in range(HEAD_PAIRS):
        x = q_ref[:, c * LANES:(c + 1) * LANES]
        qs_ref[c * Q_BLOCK:(c + 1) * Q_BLOCK, 0:LANES] = (_half_rms(x, lo, qn_ref[...]) * scale).astype(BF16)

    sc = _dot_nt(kc_ref[...], qs_ref[:, 0:LANES])
    valid_c = row * CMP_STRIDE + (CMP_BLOCK - 1) <= t_q
    psum = jnp.zeros((N_CMP_PAD, Q_BLOCK), F32)
    for c in range(HEAD_PAIRS):
        for hf in range(2):
            s = sc[hf * N_CMP_PAD:(hf + 1) * N_CMP_PAD, c * Q_BLOCK:(c + 1) * Q_BLOCK]
            s = jnp.where(valid_c, s, MASK_VALUE)
            e = jnp.where(valid_c, jnp.exp2(s - jnp.max(s, axis=0, keepdims=True)), 0.0)
            den = jnp.sum(e, axis=0, keepdims=True)
            p = e / jnp.where(den > 0.0, den, 1.0)
            psum = psum + p
            pc_ref[hf * N_CMP_PAD:(hf + 1) * N_CMP_PAD, c * Q_BLOCK:(c + 1) * Q_BLOCK] = p.astype(BF16)
    o_cmp = _dot(vc_ref[...], pc_ref[...])

    rows2 = 2 * KEY_TILE
    tail_row = lax.broadcasted_iota(jnp.int32, (V_ROWS - LANES, cols_all), 0)

    def per_head_half(vals, even, odd):
        parts = [vals[:HEAD_DIM] * even, vals[HEAD_DIM:LANES] * odd]
        if vals.shape[0] > LANES:
            parts.append(vals[LANES:] * jnp.where(tail_row == 0, even, jnp.where(tail_row == 1, odd, 0.0)))
        return jnp.concatenate(parts, axis=0)

    def normalised(acc):
        return per_head_half(acc[:LANES], 1.0 / acc[L_ROW:L_ROW + 1], 1.0 / acc[L_ROW + 1:L_ROW + 2])

    n_win = WINDOW // KEY_TILE + 1
    s_win, v_idx = [], []
    for c in range(n_win):
        kt = i - (n_win - 1) + c
        idx = jnp.maximum(kt, 0)
        key = kt * KEY_TILE + row
        lag = t_q - key
        bias = jnp.where((key >= 0) & (lag >= 0) & (lag < WINDOW), 0.0, MASK_VALUE)
        s_win.append(_dot_nt(kw_ref[idx], qs_ref[:, 0:LANES]) + jnp.concatenate([tile_cols(bias)] * 2, axis=0))
        v_idx.append(idx)
    p_w = [[None, None] for _ in range(n_win)]
    for hf in range(2):
        half = slice(hf * KEY_TILE, (hf + 1) * KEY_TILE)
        m = jnp.max(s_win[0][half], axis=0, keepdims=True)
        for c in range(1, n_win):
            m = jnp.maximum(m, jnp.max(s_win[c][half], axis=0, keepdims=True))
        for c in range(n_win):
            p_w[c][hf] = jnp.exp2(s_win[c][half] - m).astype(BF16)
    pv = None
    for c in range(n_win):
        d = _dot(vw_ref[v_idx[c]], jnp.concatenate(p_w[c], axis=0))
        pv = d if pv is None else pv + d
    o_win = normalised(pv)

    gt_ref[...] = jax.nn.sigmoid(gate_ref[...]).T

    def gate_pair(br, c):
        r = br * HEADS_PER_GROUP + 2 * c
        return jnp.concatenate([jnp.broadcast_to(gt_ref[r:r + 1, :], (HEAD_DIM, Q_BLOCK)),
                                jnp.broadcast_to(gt_ref[r + 1:r + 2, :], (HEAD_DIM, Q_BLOCK))], axis=0)

    for c in range(HEAD_PAIRS):
        cols = slice(c * Q_BLOCK, (c + 1) * Q_BLOCK)
        og_ref[:, cols] = gate_pair(0, c) * o_cmp[:, cols] + gate_pair(2, c) * o_win[:, cols]

    n_blk = N_CMP_PAD // CMP_PER_SEL
    pt_ref[0:IMP_PAD, :] = jnp.zeros((IMP_PAD, LANES), F32)
    pt_ref[IMP_PAD:, :] = psum
    imp = pt_ref[pl.ds(IMP_PAD - 1, n_blk, stride=CMP_PER_SEL), :]
    for d in range(CMP_PER_SEL):
        imp = imp + pt_ref[pl.ds(IMP_PAD + d, n_blk, stride=CMP_PER_SEL), :]
    blk = lax.broadcasted_iota(jnp.int32, (n_blk, LANES), 0)
    cur = (i * Q_BLOCK + lax.broadcasted_iota(jnp.int32, (n_blk, LANES), 1)) // SEL_BLOCK
    forced = (blk == 0) | (blk == cur) | (blk == cur - 1)
    score = jnp.where(blk > cur, -1.0, jnp.where(forced, FORCE_SCORE, imp))
    rank = jnp.zeros((n_blk, LANES), F32)
    for j in range(n_blk):
        other = jnp.broadcast_to(score[j:j + 1, :], (n_blk, LANES))
        beats = (other > score) | ((other == score) & (blk > j))
        rank = rank + jnp.where(beats, 1.0, 0.0)
    not_sel = jnp.where(rank < float(N_SEL), 0.0, 1.0)
    not_sel = jnp.concatenate([not_sel, jnp.zeros((LANES - n_blk, LANES), F32)], axis=0)
    not_sel_q = not_sel.T.astype(BF16)
    for c in range(HEAD_PAIRS):
        qs_ref[c * Q_BLOCK:(c + 1) * Q_BLOCK, LANES:] = not_sel_q

    M_E, M_O, A_E, A_O = (SUBLANES * r for r in range(4))
    st_row = lambda r: st_ref[r:r + 1, :]
    last_chunk = i // SEL_CHUNK

    def chunk_scores(r):
        for c in range(SEL_CHUNK):
            s_ref[c * rows2:(c + 1) * rows2, :] = _dot_nt(ks_ref[r * SEL_CHUNK + c], qs_ref[...])

    def chunk_pv(r):
        pv = None
        for c in range(SEL_CHUNK):
            d = _dot(vs_ref[r * SEL_CHUNK + c], p_ref[c * rows2:(c + 1) * rows2, :])
            pv = d if pv is None else pv + d
        return per_head_half(acc_ref[...], st_row(A_E), st_row(A_O)) + pv

    def chunk_softmax(causal_chunk):
        bias = None
        if causal_chunk is not None:
            bias = [tile_cols(jnp.where((causal_chunk * SEL_CHUNK + c) * KEY_TILE + row <= t_q, 0.0, MASK_VALUE))
                    for c in range(SEL_CHUNK)]
        p_new = [[None, None] for _ in range(SEL_CHUNK)]
        for hf, (m_r, a_r) in enumerate(((M_E, A_E), (M_O, A_O))):
            s = [s_ref[c * rows2 + hf * KEY_TILE:c * rows2 + (hf + 1) * KEY_TILE, :] for c in range(SEL_CHUNK)]
            if bias is not None:
                s = [s[c] + bias[c] for c in range(SEL_CHUNK)]
            m_prev = st_row(m_r)
            m_new = m_prev
            for c in range(SEL_CHUNK):
                m_new = jnp.maximum(m_new, jnp.max(s[c], axis=0, keepdims=True))
            for c in range(SEL_CHUNK):
                p_new[c][hf] = jnp.exp2(s[c] - m_new).astype(BF16)
            st_ref[m_r:m_r + 1, :] = m_new
            st_ref[a_r:a_r + 1, :] = jnp.exp2(m_prev - m_new)
        return [jnp.concatenate(p_new[c], axis=0) for c in range(SEL_CHUNK)]

    def store_p(p_new):
        for c in range(SEL_CHUNK):
            p_ref[c * rows2:(c + 1) * rows2, :] = p_new[c]

    chunk_scores(0)
    acc_ref[...] = jnp.zeros(acc_ref.shape, F32)
    p_ref[...] = jnp.zeros(p_ref.shape, BF16)
    st_ref[M_E:A_E, :] = jnp.full((2 * SUBLANES, cols_all), MASK_VALUE, F32)
    st_ref[A_E:, :] = jnp.zeros((2 * SUBLANES, cols_all), F32)

    def sel_body(r, carry):
        acc_new = chunk_pv(jnp.maximum(r - 1, 0))
        p_new = chunk_softmax(None)
        acc_ref[...] = acc_new
        store_p(p_new)
        chunk_scores(r + 1)
        return carry

    lax.fori_loop(0, last_chunk, sel_body, 0)
    acc_new = chunk_pv(jnp.maximum(last_chunk - 1, 0))
    p_new = chunk_softmax(last_chunk)
    acc_ref[...] = acc_new
    store_p(p_new)
    o_sel = normalised(chunk_pv(last_chunk))

    for c in range(HEAD_PAIRS):
        cols = slice(c * Q_BLOCK, (c + 1) * Q_BLOCK)
        o_t = og_ref[:, cols] + gate_pair(1, c) * o_sel[:, cols]
        o_ref[:, c * LANES:(c + 1) * LANES] = o_t.T


def _nsa(proj, qn, kc, vc, ks, vs, kw, vw, layer, B, S):
    n_q = S // Q_BLOCK
    gate_blk = COL_GATE // LANES
    per_group = lambda a: pl.BlockSpec((None, None) + a.shape[2:], lambda b, g, i: (b, g) + (0,) * (a.ndim - 2))
    cols_all = HEAD_PAIRS * Q_BLOCK
    return pl.pallas_call(
        _nsa_kernel,
        grid=(B, KV_GROUPS, n_q),
        in_specs=[
            pl.BlockSpec((Q_BLOCK, GROUP_WIDTH), lambda b, g, i: (b * n_q + i, g)),
            pl.BlockSpec((Q_BLOCK, LANES), lambda b, g, i: (b * n_q + i, gate_blk + g)),
            _layer_spec((1, LANES), layer, lambda b, g, i: (0, 0)),
            per_group(kc), per_group(vc), per_group(ks), per_group(vs), per_group(kw), per_group(vw),
        ],
        out_specs=pl.BlockSpec((Q_BLOCK, GROUP_WIDTH), lambda b, g, i: (b * n_q + i, g)),
        out_shape=jax.ShapeDtypeStruct((B * S, NSA_WIDTH), F32),
        scratch_shapes=[
            pltpu.VMEM((cols_all, 2 * LANES), BF16),
            pltpu.VMEM((2 * N_CMP_PAD, cols_all), BF16),
            pltpu.VMEM((IMP_PAD + N_CMP_PAD, LANES), F32),
            pltpu.VMEM((LANES, Q_BLOCK), F32),
            pltpu.VMEM((LANES, cols_all), F32),
            pltpu.VMEM((SEL_CHUNK * 2 * KEY_TILE, cols_all), F32),
            pltpu.VMEM((SEL_CHUNK * 2 * KEY_TILE, cols_all), BF16),
            pltpu.VMEM((V_ROWS, cols_all), F32),
            pltpu.VMEM((4 * SUBLANES, cols_all), F32),
        ],
        compiler_params=_cparams(("parallel", "parallel", "arbitrary")),
        name="nsa_attention",
    )(proj, proj, qn, kc, vc, ks, vs, kw, vw)


def _conv_kernel(b_ref, c_ref, x_ref, w_ref, o_ref):
    u = c_ref[...] * x_ref[...]
    row = lax.broadcasted_iota(jnp.int32, u.shape, 0)
    u1 = jnp.where(row >= 1, pltpu.roll(u, 1, 0), 0.0)
    u2 = jnp.where(row >= 2, pltpu.roll(u, 2, 0), 0.0)
    y = w_ref[2:3, :] * u + w_ref[1:2, :] * u1 + w_ref[0:1, :] * u2
    o_ref[...] = b_ref[...] * y


def _conv(proj, conv_w, layer, B, S, cw=256):
    nb = CONV_WIDTH // cw
    base = COL_CONV // cw
    return pl.pallas_call(
        _conv_kernel,
        grid=(B, nb),
        in_specs=[
            pl.BlockSpec((S, cw), lambda b, j: (b, base + j)),
            pl.BlockSpec((S, cw), lambda b, j: (b, base + nb + j)),
            pl.BlockSpec((S, cw), lambda b, j: (b, base + 2 * nb + j)),
            _layer_spec((CONV_K, cw), layer, lambda b, j: (0, j)),
        ],
        out_specs=pl.BlockSpec((S, cw), lambda b, j: (b, j)),
        out_shape=jax.ShapeDtypeStruct((B * S, CONV_WIDTH), F32),
        compiler_params=_cparams(("parallel", "parallel")),
        name="gated_conv",
    )(proj, proj, proj, conv_w)


def _mix_out_kernel(x_ref, a_ref, b_ref, ga_ref, gb_ref, wa_ref, wb_ref, o_ref):
    a = _rms(a_ref[...], ga_ref[...]).astype(BF16)
    b = _rms(b_ref[...], gb_ref[...]).astype(BF16)
    o_ref[...] = x_ref[...] + _dot(a, wa_ref[...]) + _dot(b, wb_ref[...])


def _mix_out(x, oa, ob, ga, gb, w_out, layer, tm=256):
    T = x.shape[0]
    const = lambda shape, idx: _layer_spec(shape, layer, lambda i: idx)
    return pl.pallas_call(
        _mix_out_kernel,
        grid=(T // tm,),
        in_specs=[
            pl.BlockSpec((tm, D_MODEL), lambda i: (i, 0)),
            pl.BlockSpec((tm, NSA_WIDTH), lambda i: (i, 0)),
            pl.BlockSpec((tm, CONV_WIDTH), lambda i: (i, 0)),
            const((1, NSA_WIDTH), (0, 0)), const((1, CONV_WIDTH), (0, 0)),
            const((NSA_WIDTH, D_MODEL), (0, 0)), const((CONV_WIDTH, D_MODEL), (1, 0)),
        ],
        out_specs=pl.BlockSpec((tm, D_MODEL), lambda i: (i, 0)),
        out_shape=jax.ShapeDtypeStruct((T, D_MODEL), F32),
        compiler_params=_cparams(("parallel",)),
        name="mixer_out",
    )(x, oa, ob, ga, gb, w_out, w_out)


def _xattn_kernel(x_ref, g_ref, wq_ref, qn_ref, kn_ref, kv_ref, wo_ref, o_ref, oh_ref):
    x = x_ref[...]
    h = _rms(x, g_ref[...]).astype(BF16)
    q = _dot(h, wq_ref[...])
    scale = XA_HEAD_DIM ** -0.5
    for hd in range(XA_HEADS):
        cols = slice(hd * XA_HEAD_DIM, (hd + 1) * XA_HEAD_DIM)
        qh = _rms(q[:, cols], qn_ref[...]).astype(BF16)
        kh = _rms(kv_ref[:, cols], kn_ref[...]).astype(BF16)
        vh = kv_ref[:, XA_WIDTH + hd * XA_HEAD_DIM:XA_WIDTH + (hd + 1) * XA_HEAD_DIM].astype(BF16)
        s = _dot_nt(qh, kh) * scale
        e = jnp.exp(s - jnp.max(s, axis=-1, keepdims=True))
        p = (e / jnp.sum(e, axis=-1, keepdims=True)).astype(BF16)
        oh_ref[:, cols] = _dot(p, vh).astype(BF16)
    o_ref[...] = x + _dot(oh_ref[...], wo_ref[...])


def _xattn(x, g, wq, qn, kn, kv, wo, layer, B, S, tm=256):
    M = kv.shape[0] // B
    n_t = S // tm
    const = lambda shape: _layer_spec(shape, layer, lambda b, i: (0, 0))
    return pl.pallas_call(
        _xattn_kernel,
        grid=(B, n_t),
        in_specs=[
            pl.BlockSpec((tm, D_MODEL), lambda b, i: (b * n_t + i, 0)),
            const((1, D_MODEL)), const((D_MODEL, XA_WIDTH)),
            const((1, XA_HEAD_DIM)), const((1, XA_HEAD_DIM)),
            pl.BlockSpec((M, 2 * XA_WIDTH), lambda b, i: (b, 0)),
            const((XA_WIDTH, D_MODEL)),
        ],
        out_specs=pl.BlockSpec((tm, D_MODEL), lambda b, i: (b * n_t + i, 0)),
        out_shape=jax.ShapeDtypeStruct((B * S, D_MODEL), F32),
        scratch_shapes=[pltpu.VMEM((tm, XA_WIDTH), BF16)],
        compiler_params=_cparams(("parallel", "parallel")),
        name="mem_xattn",
    )(x, g, wq, qn, kn, kv, wo)


def _dup(v):
    return jnp.concatenate([v, v], axis=-1)


def _reorder_w_in(w_in):
    L = w_in.shape[0]
    n_gate = N_BRANCH * NSA_HEADS
    kv_end = NSA_WIDTH + 6 * LANES
    w_in = w_in.astype(BF16)
    gate = w_in[:, :, kv_end:kv_end + n_gate]
    gate = gate.reshape(L, D_MODEL, N_BRANCH, KV_GROUPS, HEADS_PER_GROUP).transpose(0, 1, 3, 2, 4)
    gate = gate.reshape(L, D_MODEL, KV_GROUPS, N_BRANCH * HEADS_PER_GROUP)
    gate = jnp.pad(gate, ((0, 0), (0, 0), (0, 0), (0, LANES - N_BRANCH * HEADS_PER_GROUP)))
    gate = gate.reshape(L, D_MODEL, KV_GROUPS * LANES)
    return jnp.concatenate([w_in[:, :, :kv_end], gate, w_in[:, :, kv_end + n_gate:]], axis=-1)


def _selection_mask_tiles(S):
    j = jnp.arange(LANES)[None, None, :]
    k = (jnp.arange(S // KEY_TILE)[:, None, None] * KEY_TILE + jnp.arange(KEY_TILE)[None, :, None])
    return jnp.where(k // SEL_BLOCK == j, MASK_VALUE, 0.0).astype(BF16)


def kernel(x, mem, ffn1_norm, ffn1_w_gate, ffn1_w_up, ffn1_w_down, mix_norm, w_in, cmp_pe_k, cmp_w1_k, cmp_w2_k, cmp_pe_v, cmp_w1_v, cmp_w2_v, q_norm, k_norm, conv_w, out_norm_nsa, out_norm_conv, w_out, xattn_norm, mem_norm, xattn_w_q, xattn_w_kv, xattn_q_norm, xattn_k_norm, xattn_w_o, ffn2_norm, ffn2_w_gate, ffn2_w_up, ffn2_w_down):
    B, S, D = x.shape
    L = w_in.shape[0]
    T = B * S
    M = mem.shape[1]
    bf = lambda w: w.astype(BF16)
    row = lambda a: a.reshape(L, 1, a.shape[-1])

    up_rows, down_rows = D_MODEL // 8, D_FF // 8
    f1g, f1u, f1d = _cast_bf16(ffn1_w_gate, up_rows), _cast_bf16(ffn1_w_up, up_rows), _cast_bf16(ffn1_w_down, down_rows)
    f2g, f2u, f2d = _cast_bf16(ffn2_w_gate, up_rows), _cast_bf16(ffn2_w_up, up_rows), _cast_bf16(ffn2_w_down, down_rows)
    w_in_r = _reorder_w_in(w_in)
    w_out_b = _cast_bf16(w_out, D_MODEL // 2)
    wq_b, wkv_b, wo_b = bf(xattn_w_q), bf(xattn_w_kv), bf(xattn_w_o)
    w1k = bf(_dup(cmp_w1_k.reshape(L, CMP_BLOCK, HEAD_DIM, CMP_HIDDEN).swapaxes(2, 3)).swapaxes(2, 3))
    w1v = bf(_dup(cmp_w1_v.reshape(L, CMP_BLOCK, HEAD_DIM, CMP_HIDDEN).swapaxes(2, 3)).swapaxes(2, 3))
    w2k, w2v = bf(_dup(cmp_w2_k)), bf(_dup(cmp_w2_v))
    pek, pev = _dup(cmp_pe_k), _dup(cmp_pe_v)
    qn, kn = row(_dup(q_norm)), _dup(k_norm)
    eneg = _selection_mask_tiles(S)

    f1n, f2n, mixn = row(ffn1_norm), row(ffn2_norm), row(mix_norm)
    ona, onc = row(out_norm_nsa), row(out_norm_conv)
    xan, memn = row(xattn_norm), row(mem_norm)
    xqn, xkn = row(xattn_q_norm), row(xattn_k_norm)

    xs = x.reshape(T, D)
    mem2 = mem.reshape(B * M, D)
    for l in range(L):
        xs = _ffn(xs, f1n, f1g, f1u, f1d, l)
        proj = _norm_matmul(xs, mixn, w_in_r, l, 1024, 1024, "mixer_in")
        kc, vc, ks, vs, kw, vw = _prep(proj, pek, w1k, w2k, pev, w1v, w2v, kn, eneg, l, B, S)
        o_a = _nsa(proj, qn, kc, vc, ks, vs, kw, vw, l, B, S)
        o_b = _conv(proj, conv_w, l, B, S)
        xs = _mix_out(xs, o_a, o_b, ona, onc, w_out_b, l)
        kv = _norm_matmul(mem2, memn, wkv_b, l, 512, 1024, "mem_kv")
        xs = _xattn(xs, xan, wq_b, xqn, xkn, kv, wo_b, l, B, S)
        xs = _ffn(xs, f2n, f2g, f2u, f2d, l)
    return xs.reshape(B, S, D)
```

```python
import math

import jax
import jax.numpy as jnp
from jax import lax
from jax.experimental import pallas as pl
from jax.experimental.pallas import tpu as pltpu

F32 = jnp.float32
BF16 = jnp.bfloat16

D_MODEL = 2048
D_FF = 5632
EPS = 1e-6
MASK_VALUE = -1e30
FORCE_SCORE = 1e4
LOG2E = math.log2(math.e)

NSA_HEADS = 16
KV_GROUPS = 2
HEADS_PER_GROUP = NSA_HEADS // KV_GROUPS
HEAD_DIM = 64
NSA_WIDTH = NSA_HEADS * HEAD_DIM
GROUP_WIDTH = NSA_WIDTH // KV_GROUPS
HEAD_PAIRS = GROUP_WIDTH // 128
N_BRANCH = 3
CMP_BLOCK = 32
CMP_STRIDE = 16
CMP_HIDDEN = 4 * HEAD_DIM
SEL_BLOCK = 64
N_SEL = 8
WINDOW = 512
Q_BLOCK = 128
CONV_WIDTH = 1024
CONV_K = 3
XA_HEADS = 4
XA_HEAD_DIM = 128
XA_WIDTH = XA_HEADS * XA_HEAD_DIM

LANES = 128
SUBLANES = 8
KEY_TILE = 128
N_CMP_PAD = 128
CMP_PER_SEL = SEL_BLOCK // CMP_STRIDE
V_ROWS = LANES + 16
L_ROW = LANES

COL_Q = 0
COL_KV = NSA_WIDTH
COL_GATE = COL_KV + 6 * LANES
COL_CONV = COL_GATE + KV_GROUPS * LANES
PROJ_COLS = COL_CONV + 3 * CONV_WIDTH

VMEM_LIMIT = 56 * 1024 * 1024


def _cparams(sem):
    return pltpu.CompilerParams(dimension_semantics=sem, vmem_limit_bytes=VMEM_LIMIT)


def _rms(x, g):
    ms = jnp.mean(x * x, axis=-1, keepdims=True)
    return x * lax.rsqrt(ms + EPS) * g


def _dot(a, b):
    return jnp.dot(a, b, preferred_element_type=F32)


def _dot_nt(a, b):
    return lax.dot_general(a, b, (((1,), (1,)), ((), ())), preferred_element_type=F32)


def _layer_spec(tail_shape, layer, tail_index, single_buffer=False):
    mode = pl.Buffered(1) if single_buffer else None
    return pl.BlockSpec((None,) + tuple(tail_shape), lambda *g: (layer,) + tuple(tail_index(*g)),
                        pipeline_mode=mode)


def _cast_kernel(w_ref, o_ref):
    o_ref[...] = w_ref[...].astype(BF16)


def _cast_layer(w, layer, rows):
    _, R, C = w.shape
    return pl.pallas_call(
        _cast_kernel,
        grid=(R // rows,),
        in_specs=[_layer_spec((rows, C), layer, lambda r: (r, 0))],
        out_specs=pl.BlockSpec((rows, C), lambda r: (r, 0)),
        out_shape=jax.ShapeDtypeStruct((R, C), BF16),
        compiler_params=_cparams(("parallel",)),
        name="cast_bf16",
    )(w)


FFN_SUB = 256
FFN_TM = 512
FFN_TF = 512


def _ffn_kernel(x_ref, g_ref, wg_ref, wu_ref, wd_ref, *rest):
    if len(rest) == 2:
        next_f32, o_ref, next_bf16, h_ref = (), rest[0], (), rest[1]
    else:
        next_f32, o_ref, next_bf16, h_ref = rest[0:3], rest[3], rest[4:7], rest[7]
    j = pl.program_id(1)

    @pl.when(j == 0)
    def _():
        h_ref[...] = _rms(x_ref[...], g_ref[...]).astype(BF16)
        o_ref[...] = jnp.zeros(o_ref.shape, F32)

    h = h_ref[...]
    part = None
    for c in range(wg_ref.shape[1] // FFN_SUB):
        cols = slice(c * FFN_SUB, (c + 1) * FFN_SUB)
        a = _dot(h, wg_ref[:, cols])
        u = _dot(h, wu_ref[:, cols])
        act = (a * jax.nn.sigmoid(a) * u).astype(BF16)
        d = _dot(act, wd_ref[cols, :])
        part = d if part is None else part + d
    o_ref[...] += part
    for src, dst in zip(next_f32, next_bf16):
        dst[...] = src[...].astype(BF16)

    @pl.when(j == pl.num_programs(1) - 1)
    def _():
        o_ref[...] = x_ref[...] + 0.5 * o_ref[...]


def _ffn(x, g, layer, wg, wu, wd, next_weights=None):
    T = x.shape[0]
    n_i, n_j = T // FFN_TM, D_FF // FFN_TF
    in_specs = [
        pl.BlockSpec((FFN_TM, D_MODEL), lambda i, j: (i, 0)),
        _layer_spec((1, D_MODEL), layer, lambda i, j: (0, 0)),
        pl.BlockSpec((D_MODEL, FFN_TF), lambda i, j: (0, j)),
        pl.BlockSpec((D_MODEL, FFN_TF), lambda i, j: (0, j)),
        pl.BlockSpec((FFN_TF, D_MODEL), lambda i, j: (j, 0)),
    ]
    out_specs = [pl.BlockSpec((FFN_TM, D_MODEL), lambda i, j: (i, 0))]
    out_shape = [jax.ShapeDtypeStruct((T, D_MODEL), F32)]
    args = [x, g, wg, wu, wd]
    if next_weights is not None:
        ng, nu, nd, nl = next_weights
        up_blk = (D_MODEL // n_i, D_FF // n_j)
        down_blk = (D_FF // n_j, D_MODEL // n_i)
        in_specs += [_layer_spec(up_blk, nl, lambda i, j: (i, j)),
                     _layer_spec(up_blk, nl, lambda i, j: (i, j)),
                     _layer_spec(down_blk, nl, lambda i, j: (j, i))]
        out_specs += [pl.BlockSpec(up_blk, lambda i, j: (i, j)),
                      pl.BlockSpec(up_blk, lambda i, j: (i, j)),
                      pl.BlockSpec(down_blk, lambda i, j: (j, i))]
        out_shape += [jax.ShapeDtypeStruct((D_MODEL, D_FF), BF16), jax.ShapeDtypeStruct((D_MODEL, D_FF), BF16),
                      jax.ShapeDtypeStruct((D_FF, D_MODEL), BF16)]
        args += [ng, nu, nd]
    outs = pl.pallas_call(
        _ffn_kernel,
        grid=(n_i, n_j),
        in_specs=in_specs,
        out_specs=out_specs,
        out_shape=out_shape,
        scratch_shapes=[pltpu.VMEM((FFN_TM, D_MODEL), BF16)],
        compiler_params=_cparams(("parallel", "arbitrary")),
        name="ffn",
    )(*args)
    return outs[0], tuple(outs[1:])


def _norm_matmul_kernel(x_ref, g_ref, w_ref, o_ref, h_ref):
    @pl.when(pl.program_id(1) == 0)
    def _():
        h_ref[...] = _rms(x_ref[...], g_ref[...]).astype(BF16)

    o_ref[...] = _dot(h_ref[...], w_ref[...])


def _norm_matmul(x, g, w, layer, tm, tn, name):
    T, K = x.shape
    N = w.shape[2]
    return pl.pallas_call(
        _norm_matmul_kernel,
        grid=(T // tm, N // tn),
        in_specs=[
            pl.BlockSpec((tm, K), lambda i, j: (i, 0)),
            _layer_spec((1, K), layer, lambda i, j: (0, 0)),
            _layer_spec((K, tn), layer, lambda i, j: (0, j)),
        ],
        out_specs=pl.BlockSpec((tm, tn), lambda i, j: (i, j)),
        out_shape=jax.ShapeDtypeStruct((T, N), F32),
        scratch_shapes=[pltpu.VMEM((tm, K), BF16)],
        compiler_params=_cparams(("parallel", "arbitrary")),
        name=name,
    )(x, g, w)


def _half_rms(x, lo, gain):
    sq = x * x
    s_lo = jnp.sum(jnp.where(lo, sq, 0.0), axis=-1, keepdims=True)
    s_hi = jnp.sum(jnp.where(lo, 0.0, sq), axis=-1, keepdims=True)
    inv = jnp.where(lo, lax.rsqrt(s_lo * (1.0 / HEAD_DIM) + EPS),
                    lax.rsqrt(s_hi * (1.0 / HEAD_DIM) + EPS))
    return x * inv * gain


def _prep_kernel(ck_ref, cv_ref, s_ref, w_ref, pek_ref, w1k_ref, w2k_ref, pev_ref, w1v_ref, w2v_ref,
                 kn_ref, eneg_ref, kc_o, vc_o, ks_o, vs_o, kw_o, vw_o):
    S = cv_ref.shape[0]
    n_tiles = S // KEY_TILE
    lo = lax.broadcasted_iota(jnp.int32, (1, LANES), 1) < HEAD_DIM
    half = CMP_BLOCK // 2

    def compress(src_ref, pe_ref, w1_ref, w2_ref):
        acc_a = jnp.zeros((2 * N_CMP_PAD, CMP_HIDDEN), F32)
        acc_b = jnp.zeros((2 * N_CMP_PAD, CMP_HIDDEN), F32)
        for l in range(half):
            x = src_ref[pl.ds(l, N_CMP_PAD, stride=CMP_STRIDE), :]
            for acc_is_b, ll in ((False, l), (True, l + half)):
                xp = x + pe_ref[ll:ll + 1, :]
                x2 = jnp.concatenate([jnp.where(lo, xp, 0.0), jnp.where(lo, 0.0, xp)],
                                     axis=0).astype(BF16)
                d = _dot(x2, w1_ref[ll])
                if acc_is_b:
                    acc_b = acc_b + d
                else:
                    acc_a = acc_a + d
        hidden = acc_a + pltpu.roll(acc_b, 2 * N_CMP_PAD - 1, 0)
        act = jax.nn.gelu(hidden, approximate=True).astype(BF16)
        return _dot(act, w2_ref[...])

    kc = _half_rms(compress(ck_ref, pek_ref, w1k_ref, w2k_ref), lo, kn_ref[0:1, :])
    vc = compress(cv_ref, pev_ref, w1v_ref, w2v_ref)
    top = lax.broadcasted_iota(jnp.int32, (LANES, 1), 0) < HEAD_DIM
    for g in range(KV_GROUPS):
        rows = slice(g * N_CMP_PAD, (g + 1) * N_CMP_PAD)
        kc_o[g, 0:N_CMP_PAD, :] = jnp.where(lo, kc[rows], 0.0).astype(BF16)
        kc_o[g, N_CMP_PAD:, :] = jnp.where(lo, 0.0, kc[rows]).astype(BF16)
        vt = vc[rows].T
        vc_o[g, :, 0:N_CMP_PAD] = jnp.where(top, vt, 0.0).astype(BF16)
        vc_o[g, :, N_CMP_PAD:] = jnp.where(top, 0.0, vt).astype(BF16)

    tail_row = lax.broadcasted_iota(jnp.int32, (V_ROWS - LANES, 2 * KEY_TILE), 0)
    tail_col = lax.broadcasted_iota(jnp.int32, (V_ROWS - LANES, 2 * KEY_TILE), 1)
    v_tail = jnp.where(tail_row == tail_col // KEY_TILE, 1.0, 0.0).astype(BF16)

    def emit(src_ref, gain, k_o, v_o, with_mask):
        k = _half_rms(src_ref[:, 0:LANES], lo, gain)
        k_sw = pltpu.roll(k, HEAD_DIM, 1)
        vt = src_ref[:, LANES:2 * LANES].T.astype(BF16)
        zeros = jnp.zeros((HEAD_DIM, KEY_TILE), BF16)
        for g in range(KV_GROUPS):
            k_lo, k_hi = (k, k_sw) if g == 0 else (k_sw, k)
            shape3 = (n_tiles, KEY_TILE, LANES)
            k_o[g, :, 0:KEY_TILE, 0:LANES] = jnp.where(lo, k_lo, 0.0).astype(BF16).reshape(shape3)
            k_o[g, :, KEY_TILE:, 0:LANES] = jnp.where(lo, 0.0, k_hi).astype(BF16).reshape(shape3)
            if with_mask:
                k_o[g, :, 0:KEY_TILE, LANES:] = eneg_ref[...]
                k_o[g, :, KEY_TILE:, LANES:] = eneg_ref[...]
            for kt in range(n_tiles):
                blk = vt[g * HEAD_DIM:(g + 1) * HEAD_DIM, kt * KEY_TILE:(kt + 1) * KEY_TILE]
                v_o[g, kt, 0:HEAD_DIM, 0:KEY_TILE] = blk
                v_o[g, kt, 0:HEAD_DIM, KEY_TILE:] = zeros
                v_o[g, kt, HEAD_DIM:LANES, 0:KEY_TILE] = zeros
                v_o[g, kt, HEAD_DIM:LANES, KEY_TILE:] = blk
                v_o[g, kt, LANES:, :] = v_tail

    emit(s_ref, kn_ref[1:2, :], ks_o, vs_o, True)
    emit(w_ref, kn_ref[2:3, :], kw_o, vw_o, False)


def _prep(proj, pek, w1k, w2k, pev, w1v, w2v, kn, eneg, layer, B, S):
    n_tiles = S // KEY_TILE
    kv_blk = COL_KV // 256
    full = lambda a: _layer_spec(a.shape[1:], layer, lambda b: (0,) * (a.ndim - 1))
    per_batch = lambda *tail: (jax.ShapeDtypeStruct((B, KV_GROUPS) + tail, BF16),
                               pl.BlockSpec((None, KV_GROUPS) + tail, lambda b: (b,) + (0,) * (len(tail) + 1)))
    kc_shape, kc_spec = per_batch(2 * N_CMP_PAD, LANES)
    vc_shape, vc_spec = per_batch(LANES, 2 * N_CMP_PAD)
    ks_shape, ks_spec = per_batch(n_tiles, 2 * KEY_TILE, 2 * LANES)
    kw_shape, kw_spec = per_batch(n_tiles, 2 * KEY_TILE, LANES)
    v_shape, v_spec = per_batch(n_tiles, V_ROWS, 2 * KEY_TILE)
    return pl.pallas_call(
        _prep_kernel,
        grid=(B,),
        in_specs=[
            pl.BlockSpec((S, LANES), lambda b: (b, COL_KV // LANES)),
            pl.BlockSpec((S, LANES), lambda b: (b, COL_KV // LANES + 1)),
            pl.BlockSpec((S, 256), lambda b: (b, kv_blk + 1)),
            pl.BlockSpec((S, 256), lambda b: (b, kv_blk + 2)),
            full(pek), full(w1k), full(w2k), full(pev), full(w1v), full(w2v), full(kn),
            pl.BlockSpec(eneg.shape, lambda b: (0, 0, 0)),
        ],
        out_specs=[kc_spec, vc_spec, ks_spec, v_spec, kw_spec, v_spec],
        out_shape=[kc_shape, vc_shape, ks_shape, v_shape, kw_shape, v_shape],
        compiler_params=_cparams(("parallel",)),
        name="kv_prep",
    )(proj, proj, proj, proj, pek, w1k, w2k, pev, w1v, w2v, kn, eneg)


IMP_PAD = SUBLANES
SEL_CHUNK = 4


def _nsa_kernel(q_ref, gate_ref, qn_ref, kc_ref, vc_ref, ks_ref, vs_ref, kw_ref, vw_ref,
                o_ref, qs_ref, pc_ref, pt_ref, gt_ref, og_ref, s_ref, p_ref, acc_ref, st_ref):
    i = pl.program_id(2)
    lane = lax.broadcasted_iota(jnp.int32, (Q_BLOCK, LANES), 1)
    row = lax.broadcasted_iota(jnp.int32, (Q_BLOCK, LANES), 0)
    lo = lane < HEAD_DIM
    t_q = i * Q_BLOCK + lane
    cols_all = HEAD_PAIRS * Q_BLOCK

    def tile_cols(a):
        return jnp.concatenate([a] * HEAD_PAIRS, axis=1)

    scale = HEAD_DIM ** -0.5 * LOG2E
    for c in range(HEAD_PAIRS):
        x = q_ref[:, c * LANES:(c + 1) * LANES]
        qs_ref[c * Q_BLOCK:(c + 1) * Q_BLOCK, 0:LANES] = (_half_rms(x, lo, qn_ref[...]) * scale).astype(BF16)

    sc = _dot_nt(kc_ref[...], qs_ref[:, 0:LANES])
    valid_c = row * CMP_STRIDE + (CMP_BLOCK - 1) <= t_q
    psum = jnp.zeros((N_CMP_PAD, Q_BLOCK), F32)
    for c in range(HEAD_PAIRS):
        for hf in range(2):
            s = sc[hf * N_CMP_PAD:(hf + 1) * N_CMP_PAD, c * Q_BLOCK:(c + 1) * Q_BLOCK]
            s = jnp.where(valid_c, s, MASK_VALUE)
            e = jnp.where(valid_c, jnp.exp2(s - jnp.max(s, axis=0, keepdims=True)), 0.0)
            den = jnp.sum(e, axis=0, keepdims=True)
            p = e / jnp.where(den > 0.0, den, 1.0)
            psum = psum + p
            pc_ref[hf * N_CMP_PAD:(hf + 1) * N_CMP_PAD, c * Q_BLOCK:(c + 1) * Q_BLOCK] = p.astype(BF16)
    o_cmp = _dot(vc_ref[...], pc_ref[...])

    rows2 = 2 * KEY_TILE
    tail_row = lax.broadcasted_iota(jnp.int32, (V_ROWS - LANES, cols_all), 0)

    def per_head_half(vals, even, odd):
        parts = [vals[:HEAD_DIM] * even, vals[HEAD_DIM:LANES] * odd]
        if vals.shape[0] > LANES:
            parts.append(vals[LANES:] * jnp.where(tail_row == 0, even, jnp.where(tail_row == 1, odd, 0.0)))
        return jnp.concatenate(parts, axis=0)

    def normalised(acc):
        return per_head_half(acc[:LANES], 1.0 / acc[L_ROW:L_ROW + 1], 1.0 / acc[L_ROW + 1:L_ROW + 2])

    n_win = WINDOW // KEY_TILE + 1
    s_win, v_idx = [], []
    for c in range(n_win):
        kt = i - (n_win - 1) + c
        idx = jnp.maximum(kt, 0)
        key = kt * KEY_TILE + row
        lag = t_q - key
        bias = jnp.where((key >= 0) & (lag >= 0) & (lag < WINDOW), 0.0, MASK_VALUE)
        s_win.append(_dot_nt(kw_ref[idx], qs_ref[:, 0:LANES]) + jnp.concatenate([tile_cols(bias)] * 2, axis=0))
        v_idx.append(idx)
    p_w = [[None, None] for _ in range(n_win)]
    for hf in range(2):
        half = slice(hf * KEY_TILE, (hf + 1) * KEY_TILE)
        m = jnp.max(s_win[0][half], axis=0, keepdims=True)
        for c in range(1, n_win):
            m = jnp.maximum(m, jnp.max(s_win[c][half], axis=0, keepdims=True))
        for c in range(n_win):
            p_w[c][hf] = jnp.exp2(s_win[c][half] - m).astype(BF16)
    pv = None
    for c in range(n_win):
        d = _dot(vw_ref[v_idx[c]], jnp.concatenate(p_w[c], axis=0))
        pv = d if pv is None else pv + d
    o_win = normalised(pv)

    gt_ref[...] = jax.nn.sigmoid(gate_ref[...]).T

    def gate_pair(br, c):
        r = br * HEADS_PER_GROUP + 2 * c
        return jnp.concatenate([jnp.broadcast_to(gt_ref[r:r + 1, :], (HEAD_DIM, Q_BLOCK)),
                                jnp.broadcast_to(gt_ref[r + 1:r + 2, :], (HEAD_DIM, Q_BLOCK))], axis=0)

    for c in range(HEAD_PAIRS):
        cols = slice(c * Q_BLOCK, (c + 1) * Q_BLOCK)
        og_ref[:, cols] = gate_pair(0, c) * o_cmp[:, cols] + gate_pair(2, c) * o_win[:, cols]

    n_blk = N_CMP_PAD // CMP_PER_SEL
    pt_ref[0:IMP_PAD, :] = jnp.zeros((IMP_PAD, LANES), F32)
    pt_ref[IMP_PAD:, :] = psum
    imp = pt_ref[pl.ds(IMP_PAD - 1, n_blk, stride=CMP_PER_SEL), :]
    for d in range(CMP_PER_SEL):
        imp = imp + pt_ref[pl.ds(IMP_PAD + d, n_blk, stride=CMP_PER_SEL), :]
    blk = lax.broadcasted_iota(jnp.int32, (n_blk, LANES), 0)
    cur = (i * Q_BLOCK + lax.broadcasted_iota(jnp.int32, (n_blk, LANES), 1)) // SEL_BLOCK
    forced = (blk == 0) | (blk == cur) | (blk == cur - 1)
    score = jnp.where(blk > cur, -1.0, jnp.where(forced, FORCE_SCORE, imp))
    rank = jnp.zeros((n_blk, LANES), F32)
    for j in range(n_blk):
        other = jnp.broadcast_to(score[j:j + 1, :], (n_blk, LANES))
        beats = (other > score) | ((other == score) & (blk > j))
        rank = rank + jnp.where(beats, 1.0, 0.0)
    not_sel = jnp.where(rank < float(N_SEL), 0.0, 1.0)
    not_sel = jnp.concatenate([not_sel, jnp.zeros((LANES - n_blk, LANES), F32)], axis=0)
    not_sel_q = not_sel.T.astype(BF16)
    for c in range(HEAD_PAIRS):
        qs_ref[c * Q_BLOCK:(c + 1) * Q_BLOCK, LANES:] = not_sel_q

    M_E, M_O, A_E, A_O = (SUBLANES * r for r in range(4))
    st_row = lambda r: st_ref[r:r + 1, :]
    last_chunk = i // SEL_CHUNK

    def chunk_scores(r):
        for c in range(SEL_CHUNK):
            s_ref[c * rows2:(c + 1) * rows2, :] = _dot_nt(ks_ref[r * SEL_CHUNK + c], qs_ref[...])

    def chunk_pv(r):
        pv = None
        for c in range(SEL_CHUNK):
            d = _dot(vs_ref[r * SEL_CHUNK + c], p_ref[c * rows2:(c + 1) * rows2, :])
            pv = d if pv is None else pv + d
        return per_head_half(acc_ref[...], st_row(A_E), st_row(A_O)) + pv

    def chunk_softmax(causal_chunk):
        bias = None
        if causal_chunk is not None:
            bias = [tile_cols(jnp.where((causal_chunk * SEL_CHUNK + c) * KEY_TILE + row <= t_q, 0.0, MASK_VALUE))
                    for c in range(SEL_CHUNK)]
        p_new = [[None, None] for _ in range(SEL_CHUNK)]
        for hf, (m_r, a_r) in enumerate(((M_E, A_E), (M_O, A_O))):
            s = [s_ref[c * rows2 + hf * KEY_TILE:c * rows2 + (hf + 1) * KEY_TILE, :] for c in range(SEL_CHUNK)]
            if bias is not None:
                s = [s[c] + bias[c] for c in range(SEL_CHUNK)]
            m_prev = st_row(m_r)
            m_new = m_prev
            for c in range(SEL_CHUNK):
                m_new = jnp.maximum(m_new, jnp.max(s[c], axis=0, keepdims=True))
            for c in range(SEL_CHUNK):
                p_new[c][hf] = jnp.exp2(s[c] - m_new).astype(BF16)
            st_ref[m_r:m_r + 1, :] = m_new
            st_ref[a_r:a_r + 1, :] = jnp.exp2(m_prev - m_new)
        return [jnp.concatenate(p_new[c], axis=0) for c in range(SEL_CHUNK)]

    def store_p(p_new):
        for c in range(SEL_CHUNK):
            p_ref[c * rows2:(c + 1) * rows2, :] = p_new[c]

    chunk_scores(0)
    acc_ref[...] = jnp.zeros(acc_ref.shape, F32)
    p_ref[...] = jnp.zeros(p_ref.shape, BF16)
    st_ref[M_E:A_E, :] = jnp.full((2 * SUBLANES, cols_all), MASK_VALUE, F32)
    st_ref[A_E:, :] = jnp.zeros((2 * SUBLANES, cols_all), F32)

    def sel_body(r, carry):
        acc_new = chunk_pv(jnp.maximum(r - 1, 0))
        p_new = chunk_softmax(None)
        acc_ref[...] = acc_new
        store_p(p_new)
        chunk_scores(r + 1)
        return carry

    lax.fori_loop(0, last_chunk, sel_body, 0)
    acc_new = chunk_pv(jnp.maximum(last_chunk - 1, 0))
    p_new = chunk_softmax(last_chunk)
    acc_ref[...] = acc_new
    store_p(p_new)
    o_sel = normalised(chunk_pv(last_chunk))

    for c in range(HEAD_PAIRS):
        cols = slice(c * Q_BLOCK, (c + 1) * Q_BLOCK)
        o_t = og_ref[:, cols] + gate_pair(1, c) * o_sel[:, cols]
        o_ref[:, c * LANES:(c + 1) * LANES] = o_t.T


def _nsa(proj, qn, kc, vc, ks, vs, kw, vw, layer, B, S):
    n_q = S // Q_BLOCK
    gate_blk = COL_GATE // LANES
    per_group = lambda a: pl.BlockSpec((None, None) + a.shape[2:], lambda b, g, i: (b, g) + (0,) * (a.ndim - 2))
    cols_all = HEAD_PAIRS * Q_BLOCK
    return pl.pallas_call(
        _nsa_kernel,
        grid=(B, KV_GROUPS, n_q),
        in_specs=[
            pl.BlockSpec((Q_BLOCK, GROUP_WIDTH), lambda b, g, i: (b * n_q + i, g)),
            pl.BlockSpec((Q_BLOCK, LANES), lambda b, g, i: (b * n_q + i, gate_blk + g)),
            _layer_spec((1, LANES), layer, lambda b, g, i: (0, 0)),
            per_group(kc), per_group(vc), per_group(ks), per_group(vs), per_group(kw), per_group(vw),
        ],
        out_specs=pl.BlockSpec((Q_BLOCK, GROUP_WIDTH), lambda b, g, i: (b * n_q + i, g)),
        out_shape=jax.ShapeDtypeStruct((B * S, NSA_WIDTH), F32),
        scratch_shapes=[
            pltpu.VMEM((cols_all, 2 * LANES), BF16),
            pltpu.VMEM((2 * N_CMP_PAD, cols_all), BF16),
            pltpu.VMEM((IMP_PAD + N_CMP_PAD, LANES), F32),
            pltpu.VMEM((LANES, Q_BLOCK), F32),
            pltpu.VMEM((LANES, cols_all), F32),
            pltpu.VMEM((SEL_CHUNK * 2 * KEY_TILE, cols_all), F32),
            pltpu.VMEM((SEL_CHUNK * 2 * KEY_TILE, cols_all), BF16),
            pltpu.VMEM((V_ROWS, cols_all), F32),
            pltpu.VMEM((4 * SUBLANES, cols_all), F32),
        ],
        compiler_params=_cparams(("parallel", "parallel", "arbitrary")),
        name="nsa_attention",
    )(proj, proj, qn, kc, vc, ks, vs, kw, vw)


MIX_TM = 512
CARRY_ROWS = SUBLANES


def _mix_xattn_kernel(x_ref, a_ref, b_ref, c_ref, xv_ref, cw_ref, ga_ref, gb_ref, wa_ref, wb_ref,
                      g_ref, wq_ref, qn_ref, kn_ref, kv_ref, wo_ref, o_ref, carry_ref, oh_ref):
    @pl.when(pl.program_id(1) == 0)
    def _():
        carry_ref[...] = jnp.zeros(carry_ref.shape, F32)

    u = c_ref[...] * xv_ref[...]
    tm = u.shape[0]
    row = lax.broadcasted_iota(jnp.int32, u.shape, 0)
    prev1 = carry_ref[CARRY_ROWS - 1:CARRY_ROWS, :]
    prev2 = carry_ref[CARRY_ROWS - 2:CARRY_ROWS - 1, :]
    u1 = jnp.where(row == 0, prev1, pltpu.roll(u, 1, 0))
    u2 = jnp.where(row == 0, prev2, jnp.where(row == 1, prev1, pltpu.roll(u, 2, 0)))
    carry_ref[...] = u[tm - CARRY_ROWS:, :]
    o_b = b_ref[...] * (cw_ref[2:3, :] * u + cw_ref[1:2, :] * u1 + cw_ref[0:1, :] * u2)

    a = _rms(a_ref[...], ga_ref[...]).astype(BF16)
    b = _rms(o_b, gb_ref[...]).astype(BF16)
    x = x_ref[...] + _dot(a, wa_ref[...]) + _dot(b, wb_ref[...])

    h = _rms(x, g_ref[...]).astype(BF16)
    q = _dot(h, wq_ref[...])
    scale = XA_HEAD_DIM ** -0.5
    for hd in range(XA_HEADS):
        cols = slice(hd * XA_HEAD_DIM, (hd + 1) * XA_HEAD_DIM)
        qh = _rms(q[:, cols], qn_ref[...]).astype(BF16)
        kh = _rms(kv_ref[:, cols], kn_ref[...]).astype(BF16)
        vh = kv_ref[:, XA_WIDTH + hd * XA_HEAD_DIM:XA_WIDTH + (hd + 1) * XA_HEAD_DIM].astype(BF16)
        s = _dot_nt(qh, kh) * scale
        e = jnp.exp(s - jnp.max(s, axis=-1, keepdims=True))
        p = (e / jnp.sum(e, axis=-1, keepdims=True)).astype(BF16)
        oh_ref[:, cols] = _dot(p, vh).astype(BF16)
    o_ref[...] = x + _dot(oh_ref[...], wo_ref[...])


def _mix_xattn(x, o_a, proj, conv_w, ga, gb, w_out, g, wq, qn, kn, kv, wo, layer, B, S):
    M = kv.shape[0] // B
    n_t = S // MIX_TM
    conv_blk = COL_CONV // CONV_WIDTH
    tile = lambda width, col: pl.BlockSpec((MIX_TM, width), lambda b, i: (b * n_t + i, col))
    const = lambda shape, idx=(0, 0): _layer_spec(shape, layer, lambda b, i: idx, single_buffer=True)
    return pl.pallas_call(
        _mix_xattn_kernel,
        grid=(B, n_t),
        in_specs=[
            tile(D_MODEL, 0), tile(NSA_WIDTH, 0),
            tile(CONV_WIDTH, conv_blk), tile(CONV_WIDTH, conv_blk + 1), tile(CONV_WIDTH, conv_blk + 2),
            const((CONV_K, CONV_WIDTH)), const((1, NSA_WIDTH)), const((1, CONV_WIDTH)),
            const((NSA_WIDTH, D_MODEL)), const((CONV_WIDTH, D_MODEL), (1, 0)),
            const((1, D_MODEL)), const((D_MODEL, XA_WIDTH)),
            const((1, XA_HEAD_DIM)), const((1, XA_HEAD_DIM)),
            pl.BlockSpec((M, 2 * XA_WIDTH), lambda b, i: (b, 0)),
            const((XA_WIDTH, D_MODEL)),
        ],
        out_specs=tile(D_MODEL, 0),
        out_shape=jax.ShapeDtypeStruct((B * S, D_MODEL), F32),
        scratch_shapes=[pltpu.VMEM((CARRY_ROWS, CONV_WIDTH), F32), pltpu.VMEM((MIX_TM, XA_WIDTH), BF16)],
        compiler_params=_cparams(("parallel", "arbitrary")),
        name="mixer_out_xattn",
    )(x, o_a, proj, proj, proj, conv_w, ga, gb, w_out, w_out, g, wq, qn, kn, kv, wo)


def _dup(v):
    return jnp.concatenate([v, v], axis=-1)


def _reorder_w_in(w_in):
    L = w_in.shape[0]
    n_gate = N_BRANCH * NSA_HEADS
    kv_end = NSA_WIDTH + 6 * LANES
    w_in = w_in.astype(BF16)
    gate = w_in[:, :, kv_end:kv_end + n_gate]
    gate = gate.reshape(L, D_MODEL, N_BRANCH, KV_GROUPS, HEADS_PER_GROUP).transpose(0, 1, 3, 2, 4)
    gate = gate.reshape(L, D_MODEL, KV_GROUPS, N_BRANCH * HEADS_PER_GROUP)
    gate = jnp.pad(gate, ((0, 0), (0, 0), (0, 0), (0, LANES - N_BRANCH * HEADS_PER_GROUP)))
    gate = gate.reshape(L, D_MODEL, KV_GROUPS * LANES)
    return jnp.concatenate([w_in[:, :, :kv_end], gate, w_in[:, :, kv_end + n_gate:]], axis=-1)


def _selection_mask_tiles(S):
    j = jnp.arange(LANES)[None, None, :]
    k = (jnp.arange(S // KEY_TILE)[:, None, None] * KEY_TILE + jnp.arange(KEY_TILE)[None, :, None])
    return jnp.where(k // SEL_BLOCK == j, MASK_VALUE, 0.0).astype(BF16)


def kernel(x, mem, ffn1_norm, ffn1_w_gate, ffn1_w_up, ffn1_w_down, mix_norm, w_in, cmp_pe_k, cmp_w1_k, cmp_w2_k, cmp_pe_v, cmp_w1_v, cmp_w2_v, q_norm, k_norm, conv_w, out_norm_nsa, out_norm_conv, w_out, xattn_norm, mem_norm, xattn_w_q, xattn_w_kv, xattn_q_norm, xattn_k_norm, xattn_w_o, ffn2_norm, ffn2_w_gate, ffn2_w_up, ffn2_w_down):
    B, S, D = x.shape
    L = w_in.shape[0]
    T = B * S
    M = mem.shape[1]
    bf = lambda w: w.astype(BF16)
    row = lambda a: a.reshape(L, 1, a.shape[-1])

    up_rows, down_rows = D_MODEL // 8, D_FF // 8
    ffn_w = (_cast_layer(ffn1_w_gate, 0, up_rows), _cast_layer(ffn1_w_up, 0, up_rows),
             _cast_layer(ffn1_w_down, 0, down_rows))
    w_in_r = _reorder_w_in(w_in)
    w_out_b = bf(w_out)
    wq_b, wkv_b, wo_b = bf(xattn_w_q), bf(xattn_w_kv), bf(xattn_w_o)
    w1k = bf(_dup(cmp_w1_k.reshape(L, CMP_BLOCK, HEAD_DIM, CMP_HIDDEN).swapaxes(2, 3)).swapaxes(2, 3))
    w1v = bf(_dup(cmp_w1_v.reshape(L, CMP_BLOCK, HEAD_DIM, CMP_HIDDEN).swapaxes(2, 3)).swapaxes(2, 3))
    w2k, w2v = bf(_dup(cmp_w2_k)), bf(_dup(cmp_w2_v))
    pek, pev = _dup(cmp_pe_k), _dup(cmp_pe_v)
    qn, kn = row(_dup(q_norm)), _dup(k_norm)
    eneg = _selection_mask_tiles(S)

    f1n, f2n, mixn = row(ffn1_norm), row(ffn2_norm), row(mix_norm)
    ona, onc = row(out_norm_nsa), row(out_norm_conv)
    xan, memn = row(xattn_norm), row(mem_norm)
    xqn, xkn = row(xattn_q_norm), row(xattn_k_norm)

    xs = x.reshape(T, D)
    mem2 = mem.reshape(B * M, D)
    for l in range(L):
        xs, ffn_w = _ffn(xs, f1n, l, *ffn_w, next_weights=(ffn2_w_gate, ffn2_w_up, ffn2_w_down, l))
        proj = _norm_matmul(xs, mixn, w_in_r, l, 1024, 1024, "mixer_in")
        kc, vc, ks, vs, kw, vw = _prep(proj, pek, w1k, w2k, pev, w1v, w2v, kn, eneg, l, B, S)
        o_a = _nsa(proj, qn, kc, vc, ks, vs, kw, vw, l, B, S)
        kv = _norm_matmul(mem2, memn, wkv_b, l, 512, 1024, "mem_kv")
        xs = _mix_xattn(xs, o_a, proj, conv_w, ona, onc, w_out_b, xan, wq_b, xqn, xkn, kv, wo_b, l, B, S)
        following = (ffn1_w_gate, ffn1_w_up, ffn1_w_down, l + 1) if l + 1 < L else None
        xs, ffn_w = _ffn(xs, f2n, l, *ffn_w, next_weights=following)
    return xs.reshape(B, S, D)
```

```python
import math

import jax
import jax.numpy as jnp
from jax import lax
from jax.experimental import pallas as pl
from jax.experimental.pallas import tpu as pltpu

F32 = jnp.float32
BF16 = jnp.bfloat16

D_MODEL = 2048
D_FF = 5632
EPS = 1e-6
MASK_VALUE = -1e30
FORCE_SCORE = 1e4
LOG2E = math.log2(math.e)

NSA_HEADS = 16
KV_GROUPS = 2
HEADS_PER_GROUP = NSA_HEADS // KV_GROUPS
HEAD_DIM = 64
NSA_WIDTH = NSA_HEADS * HEAD_DIM
GROUP_WIDTH = NSA_WIDTH // KV_GROUPS
HEAD_PAIRS = GROUP_WIDTH // 128
N_BRANCH = 3
CMP_BLOCK = 32
CMP_STRIDE = 16
CMP_HIDDEN = 4 * HEAD_DIM
SEL_BLOCK = 64
N_SEL = 8
WINDOW = 512
Q_BLOCK = 128
CONV_WIDTH = 1024
CONV_K = 3
XA_HEADS = 4
XA_HEAD_DIM = 128
XA_WIDTH = XA_HEADS * XA_HEAD_DIM

LANES = 128
SUBLANES = 8
KEY_TILE = 128
N_CMP_PAD = 128
CMP_PER_SEL = SEL_BLOCK // CMP_STRIDE
V_ROWS = LANES + 16
L_ROW = LANES

COL_KV = NSA_WIDTH
COL_GATE = COL_KV + 6 * LANES
MAIN_COLS = COL_GATE + 2 * LANES
COL_CONV = MAIN_COLS
CONV_SRC = COL_GATE + N_BRANCH * NSA_HEADS
PROJ_COLS = COL_CONV + 3 * CONV_WIDTH
PROJ_TN = 1024

VMEM_LIMIT = 56 * 1024 * 1024


def _cparams(sem):
    return pltpu.CompilerParams(dimension_semantics=sem, vmem_limit_bytes=VMEM_LIMIT)


def _rms(x, g):
    ms = jnp.mean(x * x, axis=-1, keepdims=True)
    return x * lax.rsqrt(ms + EPS) * g


def _dot(a, b):
    return jnp.dot(a, b, preferred_element_type=F32)


def _dot_nt(a, b):
    return lax.dot_general(a, b, (((1,), (1,)), ((), ())), preferred_element_type=F32)


def _layer_spec(tail_shape, layer, tail_index, single_buffer=False):
    mode = pl.Buffered(1) if single_buffer else None
    return pl.BlockSpec((None,) + tuple(tail_shape), lambda *g: (layer,) + tuple(tail_index(*g)),
                        pipeline_mode=mode)


def _cast_kernel(w_ref, o_ref):
    o_ref[...] = w_ref[...].astype(BF16)


def _cast_layer(w, layer, rows):
    _, R, C = w.shape
    return pl.pallas_call(
        _cast_kernel,
        grid=(R // rows,),
        in_specs=[_layer_spec((rows, C), layer, lambda r: (r, 0))],
        out_specs=pl.BlockSpec((rows, C), lambda r: (r, 0)),
        out_shape=jax.ShapeDtypeStruct((R, C), BF16),
        compiler_params=_cparams(("parallel",)),
        name="cast_bf16",
    )(w)


FFN_SUB = 256
FFN_TM = 512
FFN_TF = 512


def _ffn_kernel(x_ref, g_ref, wg_ref, wu_ref, wd_ref, *rest):
    if len(rest) == 2:
        next_f32, o_ref, next_bf16, h_ref = (), rest[0], (), rest[1]
    else:
        next_f32, o_ref, next_bf16, h_ref = rest[0:3], rest[3], rest[4:7], rest[7]

    @pl.when(pl.program_id(1) == 0)
    def _():
        x = x_ref[...]
        h_ref[...] = _rms(x, g_ref[...]).astype(BF16)
        o_ref[...] = x

    h = h_ref[...]
    part = None
    for c in range(wg_ref.shape[1] // FFN_SUB):
        cols = slice(c * FFN_SUB, (c + 1) * FFN_SUB)
        a = _dot(h, wg_ref[:, cols])
        u = _dot(h, wu_ref[:, cols])
        act = (a * jax.nn.sigmoid(a) * (0.5 * u)).astype(BF16)
        d = _dot(act, wd_ref[cols, :])
        part = d if part is None else part + d
    o_ref[...] += part
    for src, dst in zip(next_f32, next_bf16):
        dst[...] = src[...].astype(BF16)


def _ffn(x, g, layer, wg, wu, wd, next_weights=None):
    T = x.shape[0]
    n_i, n_j = T // FFN_TM, D_FF // FFN_TF
    in_specs = [
        pl.BlockSpec((FFN_TM, D_MODEL), lambda i, j: (i, 0)),
        _layer_spec((1, D_MODEL), layer, lambda i, j: (0, 0)),
        pl.BlockSpec((D_MODEL, FFN_TF), lambda i, j: (0, j)),
        pl.BlockSpec((D_MODEL, FFN_TF), lambda i, j: (0, j)),
        pl.BlockSpec((FFN_TF, D_MODEL), lambda i, j: (j, 0)),
    ]
    out_specs = [pl.BlockSpec((FFN_TM, D_MODEL), lambda i, j: (i, 0))]
    out_shape = [jax.ShapeDtypeStruct((T, D_MODEL), F32)]
    args = [x, g, wg, wu, wd]
    if next_weights is not None:
        ng, nu, nd, nl = next_weights
        up_blk = (D_MODEL // n_i, D_FF // n_j)
        down_blk = (D_FF // n_j, D_MODEL // n_i)
        in_specs += [_layer_spec(up_blk, nl, lambda i, j: (i, j)),
                     _layer_spec(up_blk, nl, lambda i, j: (i, j)),
                     _layer_spec(down_blk, nl, lambda i, j: (j, i))]
        out_specs += [pl.BlockSpec(up_blk, lambda i, j: (i, j)),
                      pl.BlockSpec(up_blk, lambda i, j: (i, j)),
                      pl.BlockSpec(down_blk, lambda i, j: (j, i))]
        out_shape += [jax.ShapeDtypeStruct((D_MODEL, D_FF), BF16), jax.ShapeDtypeStruct((D_MODEL, D_FF), BF16),
                      jax.ShapeDtypeStruct((D_FF, D_MODEL), BF16)]
        args += [ng, nu, nd]
    outs = pl.pallas_call(
        _ffn_kernel,
        grid=(n_i, n_j),
        in_specs=in_specs,
        out_specs=out_specs,
        out_shape=out_shape,
        scratch_shapes=[pltpu.VMEM((FFN_TM, D_MODEL), BF16)],
        compiler_params=_cparams(("parallel", "arbitrary")),
        name="ffn",
    )(*args)
    return outs[0], tuple(outs[1:])


def _norm_matmul_kernel(x_ref, g_ref, w_ref, o_ref, h_ref):
    @pl.when(pl.program_id(1) == 0)
    def _():
        h_ref[...] = _rms(x_ref[...], g_ref[...]).astype(BF16)

    o_ref[...] = _dot(h_ref[...], w_ref[...])


def _norm_matmul(x, g, w, layer, tm, tn, name):
    T, K = x.shape
    N = w.shape[2]
    return pl.pallas_call(
        _norm_matmul_kernel,
        grid=(T // tm, N // tn),
        in_specs=[
            pl.BlockSpec((tm, K), lambda i, j: (i, 0)),
            _layer_spec((1, K), layer, lambda i, j: (0, 0)),
            _layer_spec((K, tn), layer, lambda i, j: (0, j)),
        ],
        out_specs=pl.BlockSpec((tm, tn), lambda i, j: (i, j)),
        out_shape=jax.ShapeDtypeStruct((T, N), F32),
        scratch_shapes=[pltpu.VMEM((tm, K), BF16)],
        compiler_params=_cparams(("parallel", "arbitrary")),
        name=name,
    )(x, g, w)


N_MAIN_TILES = MAIN_COLS // PROJ_TN


def _mixer_in_kernel(x_ref, g_ref, wm_ref, wc_ref, o_ref, h_ref):
    j = pl.program_id(1)

    @pl.when(j == 0)
    def _():
        h_ref[...] = _rms(x_ref[...], g_ref[...]).astype(BF16)

    @pl.when(j < N_MAIN_TILES)
    def _():
        o_ref[...] = _dot(h_ref[...], wm_ref[...])

    @pl.when(j >= N_MAIN_TILES)
    def _():
        o_ref[...] = _dot(h_ref[...], wc_ref[...])


def _mixer_in(x, g, w_main, w_conv, layer, tm=1024):
    T, K = x.shape
    return pl.pallas_call(
        _mixer_in_kernel,
        grid=(T // tm, PROJ_COLS // PROJ_TN),
        in_specs=[
            pl.BlockSpec((tm, K), lambda i, j: (i, 0)),
            _layer_spec((1, K), layer, lambda i, j: (0, 0)),
            _layer_spec((K, PROJ_TN), layer, lambda i, j: (0, jnp.minimum(j, N_MAIN_TILES - 1))),
            _layer_spec((K, PROJ_TN), layer, lambda i, j: (0, jnp.maximum(j - N_MAIN_TILES, 0))),
        ],
        out_specs=pl.BlockSpec((tm, PROJ_TN), lambda i, j: (i, j)),
        out_shape=jax.ShapeDtypeStruct((T, PROJ_COLS), F32),
        scratch_shapes=[pltpu.VMEM((tm, K), BF16)],
        compiler_params=_cparams(("parallel", "arbitrary")),
        name="mixer_in",
    )(x, g, w_main, w_conv)


def _half_rms(x, lo, gain):
    sq = x * x
    s_lo = jnp.sum(jnp.where(lo, sq, 0.0), axis=-1, keepdims=True)
    s_hi = jnp.sum(jnp.where(lo, 0.0, sq), axis=-1, keepdims=True)
    inv = jnp.where(lo, lax.rsqrt(s_lo * (1.0 / HEAD_DIM) + EPS),
                    lax.rsqrt(s_hi * (1.0 / HEAD_DIM) + EPS))
    return x * inv * gain


def _prep_kernel(ck_ref, cv_ref, s_ref, w_ref, pek_ref, w1k_ref, w2k_ref, pev_ref, w1v_ref, w2v_ref,
                 kn_ref, eneg_ref, kc_o, vc_o, ks_o, vs_o, kw_o, vw_o):
    S = cv_ref.shape[0]
    n_tiles = S // KEY_TILE
    lo = lax.broadcasted_iota(jnp.int32, (1, LANES), 1) < HEAD_DIM
    half = CMP_BLOCK // 2

    def compress(src_ref, pe_ref, w1_ref, w2_ref):
        acc_a = jnp.zeros((2 * N_CMP_PAD, CMP_HIDDEN), F32)
        acc_b = jnp.zeros((2 * N_CMP_PAD, CMP_HIDDEN), F32)
        for l in range(half):
            x = src_ref[pl.ds(l, N_CMP_PAD, stride=CMP_STRIDE), :]
            for acc_is_b, ll in ((False, l), (True, l + half)):
                xp = x + pe_ref[ll:ll + 1, :]
                x2 = jnp.concatenate([jnp.where(lo, xp, 0.0), jnp.where(lo, 0.0, xp)],
                                     axis=0).astype(BF16)
                d = _dot(x2, w1_ref[ll])
                if acc_is_b:
                    acc_b = acc_b + d
                else:
                    acc_a = acc_a + d
        hidden = acc_a + pltpu.roll(acc_b, 2 * N_CMP_PAD - 1, 0)
        act = jax.nn.gelu(hidden, approximate=True).astype(BF16)
        return _dot(act, w2_ref[...])

    kc = _half_rms(compress(ck_ref, pek_ref, w1k_ref, w2k_ref), lo, kn_ref[0:1, :])
    vc = compress(cv_ref, pev_ref, w1v_ref, w2v_ref)
    top = lax.broadcasted_iota(jnp.int32, (LANES, 1), 0) < HEAD_DIM
    for g in range(KV_GROUPS):
        rows = slice(g * N_CMP_PAD, (g + 1) * N_CMP_PAD)
        kc_o[g, 0:N_CMP_PAD, :] = jnp.where(lo, kc[rows], 0.0).astype(BF16)
        kc_o[g, N_CMP_PAD:, :] = jnp.where(lo, 0.0, kc[rows]).astype(BF16)
        vt = vc[rows].T
        vc_o[g, :, 0:N_CMP_PAD] = jnp.where(top, vt, 0.0).astype(BF16)
        vc_o[g, :, N_CMP_PAD:] = jnp.where(top, 0.0, vt).astype(BF16)

    tail_row = lax.broadcasted_iota(jnp.int32, (V_ROWS - LANES, 2 * KEY_TILE), 0)
    tail_col = lax.broadcasted_iota(jnp.int32, (V_ROWS - LANES, 2 * KEY_TILE), 1)
    v_tail = jnp.where(tail_row == tail_col // KEY_TILE, 1.0, 0.0).astype(BF16)

    def emit(src_ref, gain, k_o, v_o, with_mask):
        k = _half_rms(src_ref[:, 0:LANES], lo, gain)
        k_sw = pltpu.roll(k, HEAD_DIM, 1)
        vt = src_ref[:, LANES:2 * LANES].T.astype(BF16)
        zeros = jnp.zeros((HEAD_DIM, KEY_TILE), BF16)
        for g in range(KV_GROUPS):
            k_lo, k_hi = (k, k_sw) if g == 0 else (k_sw, k)
            shape3 = (n_tiles, KEY_TILE, LANES)
            k_o[g, :, 0:KEY_TILE, 0:LANES] = jnp.where(lo, k_lo, 0.0).astype(BF16).reshape(shape3)
            k_o[g, :, KEY_TILE:, 0:LANES] = jnp.where(lo, 0.0, k_hi).astype(BF16).reshape(shape3)
            if with_mask:
                k_o[g, :, 0:KEY_TILE, LANES:] = eneg_ref[...]
                k_o[g, :, KEY_TILE:, LANES:] = eneg_ref[...]
            for kt in range(n_tiles):
                blk = vt[g * HEAD_DIM:(g + 1) * HEAD_DIM, kt * KEY_TILE:(kt + 1) * KEY_TILE]
                v_o[g, kt, 0:HEAD_DIM, 0:KEY_TILE] = blk
                v_o[g, kt, 0:HEAD_DIM, KEY_TILE:] = zeros
                v_o[g, kt, HEAD_DIM:LANES, 0:KEY_TILE] = zeros
                v_o[g, kt, HEAD_DIM:LANES, KEY_TILE:] = blk
                v_o[g, kt, LANES:, :] = v_tail

    emit(s_ref, kn_ref[1:2, :], ks_o, vs_o, True)
    emit(w_ref, kn_ref[2:3, :], kw_o, vw_o, False)


def _prep(proj, pek, w1k, w2k, pev, w1v, w2v, kn, eneg, layer, B, S):
    n_tiles = S // KEY_TILE
    kv_blk = COL_KV // 256
    full = lambda a: _layer_spec(a.shape[1:], layer, lambda b: (0,) * (a.ndim - 1))
    per_batch = lambda *tail: (jax.ShapeDtypeStruct((B, KV_GROUPS) + tail, BF16),
                               pl.BlockSpec((None, KV_GROUPS) + tail, lambda b: (b,) + (0,) * (len(tail) + 1)))
    kc_shape, kc_spec = per_batch(2 * N_CMP_PAD, LANES)
    vc_shape, vc_spec = per_batch(LANES, 2 * N_CMP_PAD)
    ks_shape, ks_spec = per_batch(n_tiles, 2 * KEY_TILE, 2 * LANES)
    kw_shape, kw_spec = per_batch(n_tiles, 2 * KEY_TILE, LANES)
    v_shape, v_spec = per_batch(n_tiles, V_ROWS, 2 * KEY_TILE)
    return pl.pallas_call(
        _prep_kernel,
        grid=(B,),
        in_specs=[
            pl.BlockSpec((S, LANES), lambda b: (b, COL_KV // LANES)),
            pl.BlockSpec((S, LANES), lambda b: (b, COL_KV // LANES + 1)),
            pl.BlockSpec((S, 256), lambda b: (b, kv_blk + 1)),
            pl.BlockSpec((S, 256), lambda b: (b, kv_blk + 2)),
            full(pek), full(w1k), full(w2k), full(pev), full(w1v), full(w2v), full(kn),
            pl.BlockSpec(eneg.shape, lambda b: (0, 0, 0)),
        ],
        out_specs=[kc_spec, vc_spec, ks_spec, v_spec, kw_spec, v_spec],
        out_shape=[kc_shape, vc_shape, ks_shape, v_shape, kw_shape, v_shape],
        compiler_params=_cparams(("parallel",)),
        name="kv_prep",
    )(proj, proj, proj, proj, pek, w1k, w2k, pev, w1v, w2v, kn, eneg)


IMP_PAD = SUBLANES
SEL_CHUNK = 4


def _nsa_kernel(q_ref, gate_ref, qn_ref, kc_ref, vc_ref, ks_ref, vs_ref, kw_ref, vw_ref,
                o_ref, qs_ref, pc_ref, pt_ref, gt_ref, og_ref, s_ref, p_ref, acc_ref, st_ref):
    i = pl.program_id(2)
    lane = lax.broadcasted_iota(jnp.int32, (Q_BLOCK, LANES), 1)
    row = lax.broadcasted_iota(jnp.int32, (Q_BLOCK, LANES), 0)
    lo = lane < HEAD_DIM
    t_q = i * Q_BLOCK + lane
    cols_all = HEAD_PAIRS * Q_BLOCK

    def tile_cols(a):
        return jnp.concatenate([a] * HEAD_PAIRS, axis=1)

    scale = HEAD_DIM ** -0.5 * LOG2E
    for c in range(HEAD_PAIRS):
        x = q_ref[:, c * LANES:(c + 1) * LANES]
        qs_ref[c * Q_BLOCK:(c + 1) * Q_BLOCK, 0:LANES] = (_half_rms(x, lo, qn_ref[...]) * scale).astype(BF16)

    sc = _dot_nt(kc_ref[...], qs_ref[:, 0:LANES])
    valid_c = row * CMP_STRIDE + (CMP_BLOCK - 1) <= t_q
    psum = jnp.zeros((N_CMP_PAD, Q_BLOCK), F32)
    for c in range(HEAD_PAIRS):
        for hf in range(2):
            s = sc[hf * N_CMP_PAD:(hf + 1) * N_CMP_PAD, c * Q_BLOCK:(c + 1) * Q_BLOCK]
            s = jnp.where(valid_c, s, MASK_VALUE)
            e = jnp.where(valid_c, jnp.exp2(s - jnp.max(s, axis=0, keepdims=True)), 0.0)
            den = jnp.sum(e, axis=0, keepdims=True)
            p = e / jnp.where(den > 0.0, den, 1.0)
            psum = psum + p
            pc_ref[hf * N_CMP_PAD:(hf + 1) * N_CMP_PAD, c * Q_BLOCK:(c + 1) * Q_BLOCK] = p.astype(BF16)
    o_cmp = _dot(vc_ref[...], pc_ref[...])

    rows2 = 2 * KEY_TILE
    tail_row = lax.broadcasted_iota(jnp.int32, (V_ROWS - LANES, cols_all), 0)

    def per_head_half(vals, even, odd):
        parts = [vals[:HEAD_DIM] * even, vals[HEAD_DIM:LANES] * odd]
        if vals.shape[0] > LANES:
            parts.append(vals[LANES:] * jnp.where(tail_row == 0, even, jnp.where(tail_row == 1, odd, 0.0)))
        return jnp.concatenate(parts, axis=0)

    def normalised(acc):
        return per_head_half(acc[:LANES], 1.0 / acc[L_ROW:L_ROW + 1], 1.0 / acc[L_ROW + 1:L_ROW + 2])

    n_win = WINDOW // KEY_TILE + 1
    s_win, v_idx = [], []
    for c in range(n_win):
        kt = i - (n_win - 1) + c
        idx = jnp.maximum(kt, 0)
        key = kt * KEY_TILE + row
        lag = t_q - key
        bias = jnp.where((key >= 0) & (lag >= 0) & (lag < WINDOW), 0.0, MASK_VALUE)
        s_win.append(_dot_nt(kw_ref[idx], qs_ref[:, 0:LANES]) + jnp.concatenate([tile_cols(bias)] * 2, axis=0))
        v_idx.append(idx)
    p_w = [[None, None] for _ in range(n_win)]
    for hf in range(2):
        half = slice(hf * KEY_TILE, (hf + 1) * KEY_TILE)
        m = jnp.max(s_win[0][half], axis=0, keepdims=True)
        for c in range(1, n_win):
            m = jnp.maximum(m, jnp.max(s_win[c][half], axis=0, keepdims=True))
        for c in range(n_win):
            p_w[c][hf] = jnp.exp2(s_win[c][half] - m).astype(BF16)
    pv = None
    for c in range(n_win):
        d = _dot(vw_ref[v_idx[c]], jnp.concatenate(p_w[c], axis=0))
        pv = d if pv is None else pv + d
    o_win = normalised(pv)

    gt_ref[...] = jax.nn.sigmoid(gate_ref[...]).T

    group = pl.program_id(1)

    def gate_pair(br, c):
        r = br * NSA_HEADS + group * HEADS_PER_GROUP + 2 * c
        return jnp.concatenate([jnp.broadcast_to(gt_ref[pl.ds(r, 1), :], (HEAD_DIM, Q_BLOCK)),
                                jnp.broadcast_to(gt_ref[pl.ds(r + 1, 1), :], (HEAD_DIM, Q_BLOCK))], axis=0)

    for c in range(HEAD_PAIRS):
        cols = slice(c * Q_BLOCK, (c + 1) * Q_BLOCK)
        og_ref[:, cols] = gate_pair(0, c) * o_cmp[:, cols] + gate_pair(2, c) * o_win[:, cols]

    n_blk = N_CMP_PAD // CMP_PER_SEL
    pt_ref[0:IMP_PAD, :] = jnp.zeros((IMP_PAD, LANES), F32)
    pt_ref[IMP_PAD:, :] = psum
    imp = pt_ref[pl.ds(IMP_PAD - 1, n_blk, stride=CMP_PER_SEL), :]
    for d in range(CMP_PER_SEL):
        imp = imp + pt_ref[pl.ds(IMP_PAD + d, n_blk, stride=CMP_PER_SEL), :]
    blk = lax.broadcasted_iota(jnp.int32, (n_blk, LANES), 0)
    cur = (i * Q_BLOCK + lax.broadcasted_iota(jnp.int32, (n_blk, LANES), 1)) // SEL_BLOCK
    forced = (blk == 0) | (blk == cur) | (blk == cur - 1)
    score = jnp.where(blk > cur, -1.0, jnp.where(forced, FORCE_SCORE, imp))
    rank = jnp.zeros((n_blk, LANES), F32)
    for j in range(n_blk):
        other = jnp.broadcast_to(score[j:j + 1, :], (n_blk, LANES))
        beats = (other > score) | ((other == score) & (blk > j))
        rank = rank + jnp.where(beats, 1.0, 0.0)
    not_sel = jnp.where(rank < float(N_SEL), 0.0, 1.0)
    not_sel = jnp.concatenate([not_sel, jnp.zeros((LANES - n_blk, LANES), F32)], axis=0)
    not_sel_q = not_sel.T.astype(BF16)
    for c in range(HEAD_PAIRS):
        qs_ref[c * Q_BLOCK:(c + 1) * Q_BLOCK, LANES:] = not_sel_q

    M_E, M_O, A_E, A_O = (SUBLANES * r for r in range(4))
    st_row = lambda r: st_ref[r:r + 1, :]
    last_chunk = i // SEL_CHUNK

    def chunk_scores(r):
        for c in range(SEL_CHUNK):
            s_ref[c * rows2:(c + 1) * rows2, :] = _dot_nt(ks_ref[r * SEL_CHUNK + c], qs_ref[...])

    def chunk_pv(r):
        pv = None
        for c in range(SEL_CHUNK):
            d = _dot(vs_ref[r * SEL_CHUNK + c], p_ref[c * rows2:(c + 1) * rows2, :])
            pv = d if pv is None else pv + d
        return per_head_half(acc_ref[...], st_row(A_E), st_row(A_O)) + pv

    def chunk_softmax(causal_chunk):
        bias = None
        if causal_chunk is not None:
            bias = [tile_cols(jnp.where((causal_chunk * SEL_CHUNK + c) * KEY_TILE + row <= t_q, 0.0, MASK_VALUE))
                    for c in range(SEL_CHUNK)]
        p_new = [[None, None] for _ in range(SEL_CHUNK)]
        for hf, (m_r, a_r) in enumerate(((M_E, A_E), (M_O, A_O))):
            s = [s_ref[c * rows2 + hf * KEY_TILE:c * rows2 + (hf + 1) * KEY_TILE, :] for c in range(SEL_CHUNK)]
            if bias is not None:
                s = [s[c] + bias[c] for c in range(SEL_CHUNK)]
            m_prev = st_row(m_r)
            m_new = m_prev
            for c in range(SEL_CHUNK):
                m_new = jnp.maximum(m_new, jnp.max(s[c], axis=0, keepdims=True))
            for c in range(SEL_CHUNK):
                p_new[c][hf] = jnp.exp2(s[c] - m_new).astype(BF16)
            st_ref[m_r:m_r + 1, :] = m_new
            st_ref[a_r:a_r + 1, :] = jnp.exp2(m_prev - m_new)
        return [jnp.concatenate(p_new[c], axis=0) for c in range(SEL_CHUNK)]

    def store_p(p_new):
        for c in range(SEL_CHUNK):
            p_ref[c * rows2:(c + 1) * rows2, :] = p_new[c]

    chunk_scores(0)
    acc_ref[...] = jnp.zeros(acc_ref.shape, F32)
    p_ref[...] = jnp.zeros(p_ref.shape, BF16)
    st_ref[M_E:A_E, :] = jnp.full((2 * SUBLANES, cols_all), MASK_VALUE, F32)
    st_ref[A_E:, :] = jnp.zeros((2 * SUBLANES, cols_all), F32)

    def sel_body(r, carry):
        acc_new = chunk_pv(jnp.maximum(r - 1, 0))
        p_new = chunk_softmax(None)
        acc_ref[...] = acc_new
        store_p(p_new)
        chunk_scores(r + 1)
        return carry

    lax.fori_loop(0, last_chunk, sel_body, 0)
    acc_new = chunk_pv(jnp.maximum(last_chunk - 1, 0))
    p_new = chunk_softmax(last_chunk)
    acc_ref[...] = acc_new
    store_p(p_new)
    o_sel = normalised(chunk_pv(last_chunk))

    for c in range(HEAD_PAIRS):
        cols = slice(c * Q_BLOCK, (c + 1) * Q_BLOCK)
        o_t = og_ref[:, cols] + gate_pair(1, c) * o_sel[:, cols]
        o_ref[:, c * LANES:(c + 1) * LANES] = o_t.T


def _nsa(proj, qn, kc, vc, ks, vs, kw, vw, layer, B, S):
    n_q = S // Q_BLOCK
    gate_blk = COL_GATE // LANES
    per_group = lambda a: pl.BlockSpec((None, None) + a.shape[2:], lambda b, g, i: (b, g) + (0,) * (a.ndim - 2))
    cols_all = HEAD_PAIRS * Q_BLOCK
    return pl.pallas_call(
        _nsa_kernel,
        grid=(B, KV_GROUPS, n_q),
        in_specs=[
            pl.BlockSpec((Q_BLOCK, GROUP_WIDTH), lambda b, g, i: (b * n_q + i, g)),
            pl.BlockSpec((Q_BLOCK, LANES), lambda b, g, i: (b * n_q + i, gate_blk)),
            _layer_spec((1, LANES), layer, lambda b, g, i: (0, 0)),
            per_group(kc), per_group(vc), per_group(ks), per_group(vs), per_group(kw), per_group(vw),
        ],
        out_specs=pl.BlockSpec((Q_BLOCK, GROUP_WIDTH), lambda b, g, i: (b * n_q + i, g)),
        out_shape=jax.ShapeDtypeStruct((B * S, NSA_WIDTH), F32),
        scratch_shapes=[
            pltpu.VMEM((cols_all, 2 * LANES), BF16),
            pltpu.VMEM((2 * N_CMP_PAD, cols_all), BF16),
            pltpu.VMEM((IMP_PAD + N_CMP_PAD, LANES), F32),
            pltpu.VMEM((LANES, Q_BLOCK), F32),
            pltpu.VMEM((LANES, cols_all), F32),
            pltpu.VMEM((SEL_CHUNK * 2 * KEY_TILE, cols_all), F32),
            pltpu.VMEM((SEL_CHUNK * 2 * KEY_TILE, cols_all), BF16),
            pltpu.VMEM((V_ROWS, cols_all), F32),
            pltpu.VMEM((4 * SUBLANES, cols_all), F32),
        ],
        compiler_params=_cparams(("parallel", "parallel", "arbitrary")),
        name="nsa_attention",
    )(proj, proj, qn, kc, vc, ks, vs, kw, vw)


MIX_TM = 512
CARRY_ROWS = SUBLANES


def _mix_xattn_kernel(x_ref, a_ref, b_ref, c_ref, xv_ref, cw_ref, ga_ref, gb_ref, wa_ref, wb_ref,
                      g_ref, wq_ref, qn_ref, kn_ref, kv_ref, wo_ref, o_ref, carry_ref, oh_ref):
    @pl.when(pl.program_id(1) == 0)
    def _():
        carry_ref[...] = jnp.zeros(carry_ref.shape, F32)

    u = c_ref[...] * xv_ref[...]
    tm = u.shape[0]
    row = lax.broadcasted_iota(jnp.int32, u.shape, 0)
    prev1 = carry_ref[CARRY_ROWS - 1:CARRY_ROWS, :]
    prev2 = carry_ref[CARRY_ROWS - 2:CARRY_ROWS - 1, :]
    u1 = jnp.where(row == 0, prev1, pltpu.roll(u, 1, 0))
    u2 = jnp.where(row == 0, prev2, jnp.where(row == 1, prev1, pltpu.roll(u, 2, 0)))
    carry_ref[...] = u[tm - CARRY_ROWS:, :]
    o_b = b_ref[...] * (cw_ref[2:3, :] * u + cw_ref[1:2, :] * u1 + cw_ref[0:1, :] * u2)

    a = _rms(a_ref[...], ga_ref[...]).astype(BF16)
    b = _rms(o_b, gb_ref[...]).astype(BF16)
    x = x_ref[...] + _dot(a, wa_ref[...]) + _dot(b, wb_ref[...])

    h = _rms(x, g_ref[...]).astype(BF16)
    q = _dot(h, wq_ref[...])
    scale = XA_HEAD_DIM ** -0.5
    for hd in range(XA_HEADS):
        cols = slice(hd * XA_HEAD_DIM, (hd + 1) * XA_HEAD_DIM)
        qh = _rms(q[:, cols], qn_ref[...]).astype(BF16)
        kh = _rms(kv_ref[:, cols], kn_ref[...]).astype(BF16)
        vh = kv_ref[:, XA_WIDTH + hd * XA_HEAD_DIM:XA_WIDTH + (hd + 1) * XA_HEAD_DIM].astype(BF16)
        s = _dot_nt(qh, kh) * scale
        e = jnp.exp(s - jnp.max(s, axis=-1, keepdims=True))
        p = (e / jnp.sum(e, axis=-1, keepdims=True)).astype(BF16)
        oh_ref[:, cols] = _dot(p, vh).astype(BF16)
    o_ref[...] = x + _dot(oh_ref[...], wo_ref[...])


def _mix_xattn(x, o_a, proj, conv_w, ga, gb, w_out, g, wq, qn, kn, kv, wo, layer, B, S):
    M = kv.shape[0] // B
    n_t = S // MIX_TM
    conv_blk = COL_CONV // CONV_WIDTH
    tile = lambda width, col: pl.BlockSpec((MIX_TM, width), lambda b, i: (b * n_t + i, col))
    const = lambda shape, idx=(0, 0): _layer_spec(shape, layer, lambda b, i: idx, single_buffer=True)
    return pl.pallas_call(
        _mix_xattn_kernel,
        grid=(B, n_t),
        in_specs=[
            tile(D_MODEL, 0), tile(NSA_WIDTH, 0),
            tile(CONV_WIDTH, conv_blk), tile(CONV_WIDTH, conv_blk + 1), tile(CONV_WIDTH, conv_blk + 2),
            const((CONV_K, CONV_WIDTH)), const((1, NSA_WIDTH)), const((1, CONV_WIDTH)),
            const((NSA_WIDTH, D_MODEL)), const((CONV_WIDTH, D_MODEL), (1, 0)),
            const((1, D_MODEL)), const((D_MODEL, XA_WIDTH)),
            const((1, XA_HEAD_DIM)), const((1, XA_HEAD_DIM)),
            pl.BlockSpec((M, 2 * XA_WIDTH), lambda b, i: (b, 0)),
            const((XA_WIDTH, D_MODEL)),
        ],
        out_specs=tile(D_MODEL, 0),
        out_shape=jax.ShapeDtypeStruct((B * S, D_MODEL), F32),
        scratch_shapes=[pltpu.VMEM((CARRY_ROWS, CONV_WIDTH), F32), pltpu.VMEM((MIX_TM, XA_WIDTH), BF16)],
        compiler_params=_cparams(("parallel", "arbitrary")),
        name="mixer_out_xattn",
    )(x, o_a, proj, proj, proj, conv_w, ga, gb, w_out, w_out, g, wq, qn, kn, kv, wo)


def _dup(v):
    return jnp.concatenate([v, v], axis=-1)


def _split_w_in(w_in):
    return w_in[:, :, :MAIN_COLS].astype(BF16), w_in[:, :, CONV_SRC:].astype(BF16)


def _selection_mask_tiles(S):
    j = jnp.arange(LANES)[None, None, :]
    k = (jnp.arange(S // KEY_TILE)[:, None, None] * KEY_TILE + jnp.arange(KEY_TILE)[None, :, None])
    return jnp.where(k // SEL_BLOCK == j, MASK_VALUE, 0.0).astype(BF16)


def kernel(x, mem, ffn1_norm, ffn1_w_gate, ffn1_w_up, ffn1_w_down, mix_norm, w_in, cmp_pe_k, cmp_w1_k, cmp_w2_k, cmp_pe_v, cmp_w1_v, cmp_w2_v, q_norm, k_norm, conv_w, out_norm_nsa, out_norm_conv, w_out, xattn_norm, mem_norm, xattn_w_q, xattn_w_kv, xattn_q_norm, xattn_k_norm, xattn_w_o, ffn2_norm, ffn2_w_gate, ffn2_w_up, ffn2_w_down):
    B, S, D = x.shape
    L = w_in.shape[0]
    T = B * S
    M = mem.shape[1]
    bf = lambda w: w.astype(BF16)
    row = lambda a: a.reshape(L, 1, a.shape[-1])

    up_rows, down_rows = D_MODEL // 8, D_FF // 8
    ffn_w = (_cast_layer(ffn1_w_gate, 0, up_rows), _cast_layer(ffn1_w_up, 0, up_rows),
             _cast_layer(ffn1_w_down, 0, down_rows))
    w_main, w_conv = _split_w_in(w_in)
    w_out_b = bf(w_out)
    wq_b, wkv_b, wo_b = bf(xattn_w_q), bf(xattn_w_kv), bf(xattn_w_o)
    w1k = bf(_dup(cmp_w1_k.reshape(L, CMP_BLOCK, HEAD_DIM, CMP_HIDDEN).swapaxes(2, 3)).swapaxes(2, 3))
    w1v = bf(_dup(cmp_w1_v.reshape(L, CMP_BLOCK, HEAD_DIM, CMP_HIDDEN).swapaxes(2, 3)).swapaxes(2, 3))
    w2k, w2v = bf(_dup(cmp_w2_k)), bf(_dup(cmp_w2_v))
    pek, pev = _dup(cmp_pe_k), _dup(cmp_pe_v)
    qn, kn = row(_dup(q_norm)), _dup(k_norm)
    eneg = _selection_mask_tiles(S)

    f1n, f2n, mixn = row(ffn1_norm), row(ffn2_norm), row(mix_norm)
    ona, onc = row(out_norm_nsa), row(out_norm_conv)
    xan, memn = row(xattn_norm), row(mem_norm)
    xqn, xkn = row(xattn_q_norm), row(xattn_k_norm)

    xs = x.reshape(T, D)
    mem2 = mem.reshape(B * M, D)
    for l in range(L):
        xs, ffn_w = _ffn(xs, f1n, l, *ffn_w, next_weights=(ffn2_w_gate, ffn2_w_up, ffn2_w_down, l))
        proj = _mixer_in(xs, mixn, w_main, w_conv, l)
        kc, vc, ks, vs, kw, vw = _prep(proj, pek, w1k, w2k, pev, w1v, w2v, kn, eneg, l, B, S)
        o_a = _nsa(proj, qn, kc, vc, ks, vs, kw, vw, l, B, S)
        kv = _norm_matmul(mem2, memn, wkv_b, l, 512, 1024, "mem_kv")
        xs = _mix_xattn(xs, o_a, proj, conv_w, ona, onc, w_out_b, xan, wq_b, xqn, xkn, kv, wo_b, l, B, S)
        following = (ffn1_w_gate, ffn1_w_up, ffn1_w_down, l + 1) if l + 1 < L else None
        xs, ffn_w = _ffn(xs, f2n, l, *ffn_w, next_weights=following)
    return xs.reshape(B, S, D)
```

```python
import math

import jax
import jax.numpy as jnp
from jax import lax
from jax.experimental import pallas as pl
from jax.experimental.pallas import tpu as pltpu

F32 = jnp.float32
BF16 = jnp.bfloat16

D_MODEL = 2048
D_FF = 5632
EPS = 1e-6
MASK_VALUE = -1e30
FORCE_SCORE = 1e4
LOG2E = math.log2(math.e)

NSA_HEADS = 16
KV_GROUPS = 2
HEADS_PER_GROUP = NSA_HEADS // KV_GROUPS
HEAD_DIM = 64
NSA_WIDTH = NSA_HEADS * HEAD_DIM
GROUP_WIDTH = NSA_WIDTH // KV_GROUPS
HEAD_PAIRS = GROUP_WIDTH // 128
N_BRANCH = 3
CMP_BLOCK = 32
CMP_STRIDE = 16
CMP_HIDDEN = 4 * HEAD_DIM
SEL_BLOCK = 64
N_SEL = 8
WINDOW = 512
Q_BLOCK = 128
CONV_WIDTH = 1024
CONV_K = 3
XA_HEADS = 4
XA_HEAD_DIM = 128
XA_WIDTH = XA_HEADS * XA_HEAD_DIM

LANES = 128
SUBLANES = 8
KEY_TILE = 128
N_CMP_PAD = 128
CMP_PER_SEL = SEL_BLOCK // CMP_STRIDE
V_ROWS = LANES + 16
L_ROW = LANES

COL_KV = NSA_WIDTH
COL_GATE = COL_KV + 6 * LANES
MAIN_COLS = COL_GATE + 2 * LANES
COL_CONV = MAIN_COLS
CONV_SRC = COL_GATE + N_BRANCH * NSA_HEADS
PROJ_COLS = COL_CONV + 3 * CONV_WIDTH
PROJ_TN = 1024

VMEM_LIMIT = 56 * 1024 * 1024


def _cparams(sem):
    return pltpu.CompilerParams(dimension_semantics=sem, vmem_limit_bytes=VMEM_LIMIT)


def _rms(x, g):
    ms = jnp.mean(x * x, axis=-1, keepdims=True)
    return x * lax.rsqrt(ms + EPS) * g


def _dot(a, b):
    return jnp.dot(a, b, preferred_element_type=F32)


def _dot_nt(a, b):
    return lax.dot_general(a, b, (((1,), (1,)), ((), ())), preferred_element_type=F32)


def _layer_spec(tail_shape, layer, tail_index, single_buffer=False):
    mode = pl.Buffered(1) if single_buffer else None
    return pl.BlockSpec((None,) + tuple(tail_shape), lambda *g: (layer,) + tuple(tail_index(*g)),
                        pipeline_mode=mode)


def _cast_kernel(w_ref, o_ref):
    o_ref[...] = w_ref[...].astype(BF16)


def _cast_layer(w, layer, rows):
    _, R, C = w.shape
    return pl.pallas_call(
        _cast_kernel,
        grid=(R // rows,),
        in_specs=[_layer_spec((rows, C), layer, lambda r: (r, 0))],
        out_specs=pl.BlockSpec((rows, C), lambda r: (r, 0)),
        out_shape=jax.ShapeDtypeStruct((R, C), BF16),
        compiler_params=_cparams(("parallel",)),
        name="cast_bf16",
    )(w)


FFN_SUB = 256
FFN_TM = 512
FFN_TF = 512


def _ffn_kernel(x_ref, g_ref, wg_ref, wu_ref, wd_ref, *rest):
    if len(rest) == 2:
        next_f32, o_ref, next_bf16, h_ref = (), rest[0], (), rest[1]
    else:
        next_f32, o_ref, next_bf16, h_ref = rest[0:3], rest[3], rest[4:7], rest[7]

    @pl.when(pl.program_id(1) == 0)
    def _():
        x = x_ref[...]
        h_ref[...] = _rms(x, g_ref[...]).astype(BF16)
        o_ref[...] = x

    h = h_ref[...]
    part = None
    for c in range(wg_ref.shape[1] // FFN_SUB):
        cols = slice(c * FFN_SUB, (c + 1) * FFN_SUB)
        a = _dot(h, wg_ref[:, cols])
        u = _dot(h, wu_ref[:, cols])
        act = (a * jax.nn.sigmoid(a) * (0.5 * u)).astype(BF16)
        d = _dot(act, wd_ref[cols, :])
        part = d if part is None else part + d
    o_ref[...] += part
    for src, dst in zip(next_f32, next_bf16):
        dst[...] = src[...].astype(BF16)


def _ffn(x, g, layer, wg, wu, wd, next_weights=None):
    T = x.shape[0]
    n_i, n_j = T // FFN_TM, D_FF // FFN_TF
    in_specs = [
        pl.BlockSpec((FFN_TM, D_MODEL), lambda i, j: (i, 0)),
        _layer_spec((1, D_MODEL), layer, lambda i, j: (0, 0)),
        pl.BlockSpec((D_MODEL, FFN_TF), lambda i, j: (0, j)),
        pl.BlockSpec((D_MODEL, FFN_TF), lambda i, j: (0, j)),
        pl.BlockSpec((FFN_TF, D_MODEL), lambda i, j: (j, 0)),
    ]
    out_specs = [pl.BlockSpec((FFN_TM, D_MODEL), lambda i, j: (i, 0))]
    out_shape = [jax.ShapeDtypeStruct((T, D_MODEL), F32)]
    args = [x, g, wg, wu, wd]
    if next_weights is not None:
        ng, nu, nd, nl = next_weights
        up_blk = (D_MODEL // n_i, D_FF // n_j)
        down_blk = (D_FF // n_j, D_MODEL // n_i)
        in_specs += [_layer_spec(up_blk, nl, lambda i, j: (i, j)),
                     _layer_spec(up_blk, nl, lambda i, j: (i, j)),
                     _layer_spec(down_blk, nl, lambda i, j: (j, i))]
        out_specs += [pl.BlockSpec(up_blk, lambda i, j: (i, j)),
                      pl.BlockSpec(up_blk, lambda i, j: (i, j)),
                      pl.BlockSpec(down_blk, lambda i, j: (j, i))]
        out_shape += [jax.ShapeDtypeStruct((D_MODEL, D_FF), BF16), jax.ShapeDtypeStruct((D_MODEL, D_FF), BF16),
                      jax.ShapeDtypeStruct((D_FF, D_MODEL), BF16)]
        args += [ng, nu, nd]
    outs = pl.pallas_call(
        _ffn_kernel,
        grid=(n_i, n_j),
        in_specs=in_specs,
        out_specs=out_specs,
        out_shape=out_shape,
        scratch_shapes=[pltpu.VMEM((FFN_TM, D_MODEL), BF16)],
        compiler_params=_cparams(("parallel", "arbitrary")),
        name="ffn",
    )(*args)
    return outs[0], tuple(outs[1:])


def _norm_matmul_kernel(x_ref, g_ref, w_ref, o_ref, h_ref):
    @pl.when(pl.program_id(1) == 0)
    def _():
        h_ref[...] = _rms(x_ref[...], g_ref[...]).astype(BF16)

    o_ref[...] = _dot(h_ref[...], w_ref[...])


def _norm_matmul(x, g, w, layer, tm, tn, name):
    T, K = x.shape
    N = w.shape[2]
    return pl.pallas_call(
        _norm_matmul_kernel,
        grid=(T // tm, N // tn),
        in_specs=[
            pl.BlockSpec((tm, K), lambda i, j: (i, 0)),
            _layer_spec((1, K), layer, lambda i, j: (0, 0)),
            _layer_spec((K, tn), layer, lambda i, j: (0, j)),
        ],
        out_specs=pl.BlockSpec((tm, tn), lambda i, j: (i, j)),
        out_shape=jax.ShapeDtypeStruct((T, N), F32),
        scratch_shapes=[pltpu.VMEM((tm, K), BF16)],
        compiler_params=_cparams(("parallel", "arbitrary")),
        name=name,
    )(x, g, w)


def _w_in_layout_kernel(w_ref, o_ref):
    o_ref[:, :MAIN_COLS] = w_ref[:, :MAIN_COLS].astype(BF16)
    o_ref[:, MAIN_COLS:] = w_ref[:, CONV_SRC:].astype(BF16)


def _w_in_layout(w_in, rows=256):
    L, D, C = w_in.shape
    return pl.pallas_call(
        _w_in_layout_kernel,
        grid=(L, D // rows),
        in_specs=[pl.BlockSpec((None, rows, C), lambda l, r: (l, r, 0))],
        out_specs=pl.BlockSpec((None, rows, PROJ_COLS), lambda l, r: (l, r, 0)),
        out_shape=jax.ShapeDtypeStruct((L, D, PROJ_COLS), BF16),
        compiler_params=_cparams(("parallel", "parallel")),
        name="w_in_layout",
    )(w_in)


def _half_rms(x, lo, gain):
    sq = x * x
    s_lo = jnp.sum(jnp.where(lo, sq, 0.0), axis=-1, keepdims=True)
    s_hi = jnp.sum(jnp.where(lo, 0.0, sq), axis=-1, keepdims=True)
    inv = jnp.where(lo, lax.rsqrt(s_lo * (1.0 / HEAD_DIM) + EPS),
                    lax.rsqrt(s_hi * (1.0 / HEAD_DIM) + EPS))
    return x * inv * gain


def _prep_kernel(ck_ref, cv_ref, s_ref, w_ref, pek_ref, w1k_ref, w2k_ref, pev_ref, w1v_ref, w2v_ref,
                 kn_ref, eneg_ref, kc_o, vc_o, ks_o, vs_o, kw_o, vw_o):
    S = cv_ref.shape[0]
    n_tiles = S // KEY_TILE
    lo = lax.broadcasted_iota(jnp.int32, (1, LANES), 1) < HEAD_DIM
    half = CMP_BLOCK // 2

    def compress(src_ref, pe_ref, w1_ref, w2_ref):
        acc_a = jnp.zeros((2 * N_CMP_PAD, CMP_HIDDEN), F32)
        acc_b = jnp.zeros((2 * N_CMP_PAD, CMP_HIDDEN), F32)
        for l in range(half):
            x = src_ref[pl.ds(l, N_CMP_PAD, stride=CMP_STRIDE), :]
            for acc_is_b, ll in ((False, l), (True, l + half)):
                xp = x + pe_ref[ll:ll + 1, :]
                x2 = jnp.concatenate([jnp.where(lo, xp, 0.0), jnp.where(lo, 0.0, xp)],
                                     axis=0).astype(BF16)
                d = _dot(x2, w1_ref[ll])
                if acc_is_b:
                    acc_b = acc_b + d
                else:
                    acc_a = acc_a + d
        hidden = acc_a + pltpu.roll(acc_b, 2 * N_CMP_PAD - 1, 0)
        act = jax.nn.gelu(hidden, approximate=True).astype(BF16)
        return _dot(act, w2_ref[...])

    kc = _half_rms(compress(ck_ref, pek_ref, w1k_ref, w2k_ref), lo, kn_ref[0:1, :])
    vc = compress(cv_ref, pev_ref, w1v_ref, w2v_ref)
    top = lax.broadcasted_iota(jnp.int32, (LANES, 1), 0) < HEAD_DIM
    for g in range(KV_GROUPS):
        rows = slice(g * N_CMP_PAD, (g + 1) * N_CMP_PAD)
        kc_o[g, 0:N_CMP_PAD, :] = jnp.where(lo, kc[rows], 0.0).astype(BF16)
        kc_o[g, N_CMP_PAD:, :] = jnp.where(lo, 0.0, kc[rows]).astype(BF16)
        vt = vc[rows].T
        vc_o[g, :, 0:N_CMP_PAD] = jnp.where(top, vt, 0.0).astype(BF16)
        vc_o[g, :, N_CMP_PAD:] = jnp.where(top, 0.0, vt).astype(BF16)

    tail_row = lax.broadcasted_iota(jnp.int32, (V_ROWS - LANES, 2 * KEY_TILE), 0)
    tail_col = lax.broadcasted_iota(jnp.int32, (V_ROWS - LANES, 2 * KEY_TILE), 1)
    v_tail = jnp.where(tail_row == tail_col // KEY_TILE, 1.0, 0.0).astype(BF16)

    def emit(src_ref, gain, k_o, v_o, with_mask):
        k = _half_rms(src_ref[:, 0:LANES], lo, gain)
        k_sw = pltpu.roll(k, HEAD_DIM, 1)
        vt = src_ref[:, LANES:2 * LANES].T.astype(BF16)
        zeros = jnp.zeros((HEAD_DIM, KEY_TILE), BF16)
        for g in range(KV_GROUPS):
            k_lo, k_hi = (k, k_sw) if g == 0 else (k_sw, k)
            shape3 = (n_tiles, KEY_TILE, LANES)
            k_o[g, :, 0:KEY_TILE, 0:LANES] = jnp.where(lo, k_lo, 0.0).astype(BF16).reshape(shape3)
            k_o[g, :, KEY_TILE:, 0:LANES] = jnp.where(lo, 0.0, k_hi).astype(BF16).reshape(shape3)
            if with_mask:
                k_o[g, :, 0:KEY_TILE, LANES:] = eneg_ref[...]
                k_o[g, :, KEY_TILE:, LANES:] = eneg_ref[...]
            for kt in range(n_tiles):
                blk = vt[g * HEAD_DIM:(g + 1) * HEAD_DIM, kt * KEY_TILE:(kt + 1) * KEY_TILE]
                v_o[g, kt, 0:HEAD_DIM, 0:KEY_TILE] = blk
                v_o[g, kt, 0:HEAD_DIM, KEY_TILE:] = zeros
                v_o[g, kt, HEAD_DIM:LANES, 0:KEY_TILE] = zeros
                v_o[g, kt, HEAD_DIM:LANES, KEY_TILE:] = blk
                v_o[g, kt, LANES:, :] = v_tail

    emit(s_ref, kn_ref[1:2, :], ks_o, vs_o, True)
    emit(w_ref, kn_ref[2:3, :], kw_o, vw_o, False)


def _prep(proj, pek, w1k, w2k, pev, w1v, w2v, kn, eneg, layer, B, S):
    n_tiles = S // KEY_TILE
    kv_blk = COL_KV // 256
    full = lambda a: _layer_spec(a.shape[1:], layer, lambda b: (0,) * (a.ndim - 1))
    per_batch = lambda *tail: (jax.ShapeDtypeStruct((B, KV_GROUPS) + tail, BF16),
                               pl.BlockSpec((None, KV_GROUPS) + tail, lambda b: (b,) + (0,) * (len(tail) + 1)))
    kc_shape, kc_spec = per_batch(2 * N_CMP_PAD, LANES)
    vc_shape, vc_spec = per_batch(LANES, 2 * N_CMP_PAD)
    ks_shape, ks_spec = per_batch(n_tiles, 2 * KEY_TILE, 2 * LANES)
    kw_shape, kw_spec = per_batch(n_tiles, 2 * KEY_TILE, LANES)
    v_shape, v_spec = per_batch(n_tiles, V_ROWS, 2 * KEY_TILE)
    return pl.pallas_call(
        _prep_kernel,
        grid=(B,),
        in_specs=[
            pl.BlockSpec((S, LANES), lambda b: (b, COL_KV // LANES)),
            pl.BlockSpec((S, LANES), lambda b: (b, COL_KV // LANES + 1)),
            pl.BlockSpec((S, 256), lambda b: (b, kv_blk + 1)),
            pl.BlockSpec((S, 256), lambda b: (b, kv_blk + 2)),
            full(pek), full(w1k), full(w2k), full(pev), full(w1v), full(w2v), full(kn),
            pl.BlockSpec(eneg.shape, lambda b: (0, 0, 0)),
        ],
        out_specs=[kc_spec, vc_spec, ks_spec, v_spec, kw_spec, v_spec],
        out_shape=[kc_shape, vc_shape, ks_shape, v_shape, kw_shape, v_shape],
        compiler_params=_cparams(("parallel",)),
        name="kv_prep",
    )(proj, proj, proj, proj, pek, w1k, w2k, pev, w1v, w2v, kn, eneg)


IMP_PAD = SUBLANES
SEL_CHUNK = 4


def _nsa_group(g, i, q_ref, gt_ref, qn_ref, kc_ref, vc_ref, ks_ref, vs_ref, kw_ref, vw_ref,
               o_ref, qs_ref, pc_ref, pt_ref, og_ref, s_ref, p_ref, acc_ref, st_ref):
    col0 = g * GROUP_WIDTH
    lane = lax.broadcasted_iota(jnp.int32, (Q_BLOCK, LANES), 1)
    row = lax.broadcasted_iota(jnp.int32, (Q_BLOCK, LANES), 0)
    lo = lane < HEAD_DIM
    t_q = i * Q_BLOCK + lane
    cols_all = HEAD_PAIRS * Q_BLOCK

    def tile_cols(a):
        return jnp.concatenate([a] * HEAD_PAIRS, axis=1)

    scale = HEAD_DIM ** -0.5 * LOG2E
    for c in range(HEAD_PAIRS):
        x = q_ref[:, col0 + c * LANES:col0 + (c + 1) * LANES]
        qs_ref[c * Q_BLOCK:(c + 1) * Q_BLOCK, 0:LANES] = (_half_rms(x, lo, qn_ref[...]) * scale).astype(BF16)

    sc = _dot_nt(kc_ref[...], qs_ref[:, 0:LANES])
    valid_c = row * CMP_STRIDE + (CMP_BLOCK - 1) <= t_q
    psum = jnp.zeros((N_CMP_PAD, Q_BLOCK), F32)
    for c in range(HEAD_PAIRS):
        for hf in range(2):
            s = sc[hf * N_CMP_PAD:(hf + 1) * N_CMP_PAD, c * Q_BLOCK:(c + 1) * Q_BLOCK]
            s = jnp.where(valid_c, s, MASK_VALUE)
            e = jnp.where(valid_c, jnp.exp2(s - jnp.max(s, axis=0, keepdims=True)), 0.0)
            den = jnp.sum(e, axis=0, keepdims=True)
            p = e / jnp.where(den > 0.0, den, 1.0)
            psum = psum + p
            pc_ref[hf * N_CMP_PAD:(hf + 1) * N_CMP_PAD, c * Q_BLOCK:(c + 1) * Q_BLOCK] = p.astype(BF16)
    o_cmp = _dot(vc_ref[...], pc_ref[...])

    rows2 = 2 * KEY_TILE
    tail_row = lax.broadcasted_iota(jnp.int32, (V_ROWS - LANES, cols_all), 0)

    def per_head_half(vals, even, odd):
        parts = [vals[:HEAD_DIM] * even, vals[HEAD_DIM:LANES] * odd]
        if vals.shape[0] > LANES:
            parts.append(vals[LANES:] * jnp.where(tail_row == 0, even, jnp.where(tail_row == 1, odd, 0.0)))
        return jnp.concatenate(parts, axis=0)

    def normalised(acc):
        return per_head_half(acc[:LANES], 1.0 / acc[L_ROW:L_ROW + 1], 1.0 / acc[L_ROW + 1:L_ROW + 2])

    n_win = WINDOW // KEY_TILE + 1
    s_win, v_idx = [], []
    for c in range(n_win):
        kt = i - (n_win - 1) + c
        idx = jnp.maximum(kt, 0)
        key = kt * KEY_TILE + row
        lag = t_q - key
        bias = jnp.where((key >= 0) & (lag >= 0) & (lag < WINDOW), 0.0, MASK_VALUE)
        s_win.append(_dot_nt(kw_ref[idx], qs_ref[:, 0:LANES]) + jnp.concatenate([tile_cols(bias)] * 2, axis=0))
        v_idx.append(idx)
    p_w = [[None, None] for _ in range(n_win)]
    for hf in range(2):
        half = slice(hf * KEY_TILE, (hf + 1) * KEY_TILE)
        m = jnp.max(s_win[0][half], axis=0, keepdims=True)
        for c in range(1, n_win):
            m = jnp.maximum(m, jnp.max(s_win[c][half], axis=0, keepdims=True))
        for c in range(n_win):
            p_w[c][hf] = jnp.exp2(s_win[c][half] - m).astype(BF16)
    pv = None
    for c in range(n_win):
        d = _dot(vw_ref[v_idx[c]], jnp.concatenate(p_w[c], axis=0))
        pv = d if pv is None else pv + d
    o_win = normalised(pv)

    def gate_pair(br, c):
        r = br * NSA_HEADS + g * HEADS_PER_GROUP + 2 * c
        return jnp.concatenate([jnp.broadcast_to(gt_ref[r:r + 1, :], (HEAD_DIM, Q_BLOCK)),
                                jnp.broadcast_to(gt_ref[r + 1:r + 2, :], (HEAD_DIM, Q_BLOCK))], axis=0)

    for c in range(HEAD_PAIRS):
        cols = slice(c * Q_BLOCK, (c + 1) * Q_BLOCK)
        og_ref[:, cols] = gate_pair(0, c) * o_cmp[:, cols] + gate_pair(2, c) * o_win[:, cols]

    n_blk = N_CMP_PAD // CMP_PER_SEL
    pt_ref[0:IMP_PAD, :] = jnp.zeros((IMP_PAD, LANES), F32)
    pt_ref[IMP_PAD:, :] = psum
    imp = pt_ref[pl.ds(IMP_PAD - 1, n_blk, stride=CMP_PER_SEL), :]
    for d in range(CMP_PER_SEL):
        imp = imp + pt_ref[pl.ds(IMP_PAD + d, n_blk, stride=CMP_PER_SEL), :]
    blk = lax.broadcasted_iota(jnp.int32, (n_blk, LANES), 0)
    cur = (i * Q_BLOCK + lax.broadcasted_iota(jnp.int32, (n_blk, LANES), 1)) // SEL_BLOCK
    forced = (blk == 0) | (blk == cur) | (blk == cur - 1)
    score = jnp.where(blk > cur, -1.0, jnp.where(forced, FORCE_SCORE, imp))
    rank = jnp.zeros((n_blk, LANES), F32)
    for j in range(n_blk):
        other = jnp.broadcast_to(score[j:j + 1, :], (n_blk, LANES))
        beats = (other > score) | ((other == score) & (blk > j))
        rank = rank + jnp.where(beats, 1.0, 0.0)
    not_sel = jnp.where(rank < float(N_SEL), 0.0, 1.0)
    not_sel = jnp.concatenate([not_sel, jnp.zeros((LANES - n_blk, LANES), F32)], axis=0)
    not_sel_q = not_sel.T.astype(BF16)
    for c in range(HEAD_PAIRS):
        qs_ref[c * Q_BLOCK:(c + 1) * Q_BLOCK, LANES:] = not_sel_q

    M_E, M_O, A_E, A_O = (SUBLANES * r for r in range(4))
    st_row = lambda r: st_ref[r:r + 1, :]
    last_chunk = i // SEL_CHUNK

    def chunk_scores(r):
        for c in range(SEL_CHUNK):
            s_ref[c * rows2:(c + 1) * rows2, :] = _dot_nt(ks_ref[r * SEL_CHUNK + c], qs_ref[...])

    def chunk_pv(r):
        pv = None
        for c in range(SEL_CHUNK):
            d = _dot(vs_ref[r * SEL_CHUNK + c], p_ref[c * rows2:(c + 1) * rows2, :])
            pv = d if pv is None else pv + d
        return per_head_half(acc_ref[...], st_row(A_E), st_row(A_O)) + pv

    def chunk_softmax(causal_chunk):
        bias = None
        if causal_chunk is not None:
            bias = [tile_cols(jnp.where((causal_chunk * SEL_CHUNK + c) * KEY_TILE + row <= t_q, 0.0, MASK_VALUE))
                    for c in range(SEL_CHUNK)]
        p_new = [[None, None] for _ in range(SEL_CHUNK)]
        for hf, (m_r, a_r) in enumerate(((M_E, A_E), (M_O, A_O))):
            s = [s_ref[c * rows2 + hf * KEY_TILE:c * rows2 + (hf + 1) * KEY_TILE, :] for c in range(SEL_CHUNK)]
            if bias is not None:
                s = [s[c] + bias[c] for c in range(SEL_CHUNK)]
            m_prev = st_row(m_r)
            m_new = m_prev
            for c in range(SEL_CHUNK):
                m_new = jnp.maximum(m_new, jnp.max(s[c], axis=0, keepdims=True))
            for c in range(SEL_CHUNK):
                p_new[c][hf] = jnp.exp2(s[c] - m_new).astype(BF16)
            st_ref[m_r:m_r + 1, :] = m_new
            st_ref[a_r:a_r + 1, :] = jnp.exp2(m_prev - m_new)
        return [jnp.concatenate(p_new[c], axis=0) for c in range(SEL_CHUNK)]

    def store_p(p_new):
        for c in range(SEL_CHUNK):
            p_ref[c * rows2:(c + 1) * rows2, :] = p_new[c]

    chunk_scores(0)
    acc_ref[...] = jnp.zeros(acc_ref.shape, F32)
    p_ref[...] = jnp.zeros(p_ref.shape, BF16)
    st_ref[M_E:A_E, :] = jnp.full((2 * SUBLANES, cols_all), MASK_VALUE, F32)
    st_ref[A_E:, :] = jnp.zeros((2 * SUBLANES, cols_all), F32)

    def loop_step(r):
        acc_new = chunk_pv(jnp.maximum(r - 1, 0))
        p_new = chunk_softmax(None)
        acc_ref[...] = acc_new
        store_p(p_new)
        chunk_scores(r + 1)

    def finish():
        acc_new = chunk_pv(jnp.maximum(last_chunk - 1, 0))
        p_new = chunk_softmax(last_chunk)
        acc_ref[...] = acc_new
        store_p(p_new)
        o_sel = normalised(chunk_pv(last_chunk))
        for c in range(HEAD_PAIRS):
            cols = slice(c * Q_BLOCK, (c + 1) * Q_BLOCK)
            o_t = og_ref[:, cols] + gate_pair(1, c) * o_sel[:, cols]
            o_ref[:, col0 + c * LANES:col0 + (c + 1) * LANES] = o_t.T

    return loop_step, finish


def _nsa_kernel(q_ref, gate_ref, qn_ref, kc_ref, vc_ref, ks_ref, vs_ref, kw_ref, vw_ref,
                o_ref, gt_ref, *group_scratch):
    i = pl.program_id(1)
    gt_ref[...] = jax.nn.sigmoid(gate_ref[...]).T
    groups = [_nsa_group(g, i, q_ref, gt_ref, qn_ref, kc_ref.at[g], vc_ref.at[g], ks_ref.at[g], vs_ref.at[g],
                         kw_ref.at[g], vw_ref.at[g], o_ref, *(ref.at[g] for ref in group_scratch))
              for g in range(KV_GROUPS)]

    def body(r, carry):
        for loop_step, _ in groups:
            loop_step(r)
        return carry

    lax.fori_loop(0, i // SEL_CHUNK, body, 0)
    for _, finish in groups:
        finish()


def _nsa(proj, qn, kc, vc, ks, vs, kw, vw, layer, B, S):
    n_q = S // Q_BLOCK
    gate_blk = COL_GATE // LANES
    per_batch = lambda a: pl.BlockSpec((None,) + a.shape[1:], lambda b, i: (b,) + (0,) * (a.ndim - 1))
    cols_all = HEAD_PAIRS * Q_BLOCK
    per_group = lambda shape, dtype: pltpu.VMEM((KV_GROUPS,) + shape, dtype)
    return pl.pallas_call(
        _nsa_kernel,
        grid=(B, n_q),
        in_specs=[
            pl.BlockSpec((Q_BLOCK, NSA_WIDTH), lambda b, i: (b * n_q + i, 0)),
            pl.BlockSpec((Q_BLOCK, LANES), lambda b, i: (b * n_q + i, gate_blk)),
            _layer_spec((1, LANES), layer, lambda b, i: (0, 0)),
            per_batch(kc), per_batch(vc), per_batch(ks), per_batch(vs), per_batch(kw), per_batch(vw),
        ],
        out_specs=pl.BlockSpec((Q_BLOCK, NSA_WIDTH), lambda b, i: (b * n_q + i, 0)),
        out_shape=jax.ShapeDtypeStruct((B * S, NSA_WIDTH), F32),
        scratch_shapes=[
            pltpu.VMEM((LANES, Q_BLOCK), F32),
            per_group((cols_all, 2 * LANES), BF16),
            per_group((2 * N_CMP_PAD, cols_all), BF16),
            per_group((IMP_PAD + N_CMP_PAD, LANES), F32),
            per_group((LANES, cols_all), F32),
            per_group((SEL_CHUNK * 2 * KEY_TILE, cols_all), F32),
            per_group((SEL_CHUNK * 2 * KEY_TILE, cols_all), BF16),
            per_group((V_ROWS, cols_all), F32),
            per_group((4 * SUBLANES, cols_all), F32),
        ],
        compiler_params=_cparams(("parallel", "arbitrary")),
        name="nsa_attention",
    )(proj, proj, qn, kc, vc, ks, vs, kw, vw)


MIX_TM = 512
CARRY_ROWS = SUBLANES


def _mix_xattn_kernel(x_ref, a_ref, b_ref, c_ref, xv_ref, cw_ref, ga_ref, gb_ref, wa_ref, wb_ref,
                      g_ref, wq_ref, qn_ref, kn_ref, kv_ref, wo_ref, o_ref, carry_ref, oh_ref):
    @pl.when(pl.program_id(1) == 0)
    def _():
        carry_ref[...] = jnp.zeros(carry_ref.shape, F32)

    u = c_ref[...] * xv_ref[...]
    tm = u.shape[0]
    row = lax.broadcasted_iota(jnp.int32, u.shape, 0)
    prev1 = carry_ref[CARRY_ROWS - 1:CARRY_ROWS, :]
    prev2 = carry_ref[CARRY_ROWS - 2:CARRY_ROWS - 1, :]
    u1 = jnp.where(row == 0, prev1, pltpu.roll(u, 1, 0))
    u2 = jnp.where(row == 0, prev2, jnp.where(row == 1, prev1, pltpu.roll(u, 2, 0)))
    carry_ref[...] = u[tm - CARRY_ROWS:, :]
    o_b = b_ref[...] * (cw_ref[2:3, :] * u + cw_ref[1:2, :] * u1 + cw_ref[0:1, :] * u2)

    a = _rms(a_ref[...], ga_ref[...]).astype(BF16)
    b = _rms(o_b, gb_ref[...]).astype(BF16)
    x = x_ref[...] + _dot(a, wa_ref[...]) + _dot(b, wb_ref[...])

    h = _rms(x, g_ref[...]).astype(BF16)
    q = _dot(h, wq_ref[...])
    scale = XA_HEAD_DIM ** -0.5
    for hd in range(XA_HEADS):
        cols = slice(hd * XA_HEAD_DIM, (hd + 1) * XA_HEAD_DIM)
        qh = _rms(q[:, cols], qn_ref[...]).astype(BF16)
        kh = _rms(kv_ref[:, cols], kn_ref[...]).astype(BF16)
        vh = kv_ref[:, XA_WIDTH + hd * XA_HEAD_DIM:XA_WIDTH + (hd + 1) * XA_HEAD_DIM].astype(BF16)
        s = _dot_nt(qh, kh) * scale
        e = jnp.exp(s - jnp.max(s, axis=-1, keepdims=True))
        p = (e / jnp.sum(e, axis=-1, keepdims=True)).astype(BF16)
        oh_ref[:, cols] = _dot(p, vh).astype(BF16)
    o_ref[...] = x + _dot(oh_ref[...], wo_ref[...])


def _mix_xattn(x, o_a, proj, conv_w, ga, gb, w_out, g, wq, qn, kn, kv, wo, layer, B, S):
    M = kv.shape[0] // B
    n_t = S // MIX_TM
    conv_blk = COL_CONV // CONV_WIDTH
    tile = lambda width, col: pl.BlockSpec((MIX_TM, width), lambda b, i: (b * n_t + i, col))
    const = lambda shape, idx=(0, 0): _layer_spec(shape, layer, lambda b, i: idx, single_buffer=True)
    return pl.pallas_call(
        _mix_xattn_kernel,
        grid=(B, n_t),
        in_specs=[
            tile(D_MODEL, 0), tile(NSA_WIDTH, 0),
            tile(CONV_WIDTH, conv_blk), tile(CONV_WIDTH, conv_blk + 1), tile(CONV_WIDTH, conv_blk + 2),
            const((CONV_K, CONV_WIDTH)), const((1, NSA_WIDTH)), const((1, CONV_WIDTH)),
            const((NSA_WIDTH, D_MODEL)), const((CONV_WIDTH, D_MODEL), (1, 0)),
            const((1, D_MODEL)), const((D_MODEL, XA_WIDTH)),
            const((1, XA_HEAD_DIM)), const((1, XA_HEAD_DIM)),
            pl.BlockSpec((M, 2 * XA_WIDTH), lambda b, i: (b, 0)),
            const((XA_WIDTH, D_MODEL)),
        ],
        out_specs=tile(D_MODEL, 0),
        out_shape=jax.ShapeDtypeStruct((B * S, D_MODEL), F32),
        scratch_shapes=[pltpu.VMEM((CARRY_ROWS, CONV_WIDTH), F32), pltpu.VMEM((MIX_TM, XA_WIDTH), BF16)],
        compiler_params=_cparams(("parallel", "arbitrary")),
        name="mixer_out_xattn",
    )(x, o_a, proj, proj, proj, conv_w, ga, gb, w_out, w_out, g, wq, qn, kn, kv, wo)


def _dup(v):
    return jnp.concatenate([v, v], axis=-1)


def _selection_mask_tiles(S):
    j = jnp.arange(LANES)[None, None, :]
    k = (jnp.arange(S // KEY_TILE)[:, None, None] * KEY_TILE + jnp.arange(KEY_TILE)[None, :, None])
    return jnp.where(k // SEL_BLOCK == j, MASK_VALUE, 0.0).astype(BF16)


def kernel(x, mem, ffn1_norm, ffn1_w_gate, ffn1_w_up, ffn1_w_down, mix_norm, w_in, cmp_pe_k, cmp_w1_k, cmp_w2_k, cmp_pe_v, cmp_w1_v, cmp_w2_v, q_norm, k_norm, conv_w, out_norm_nsa, out_norm_conv, w_out, xattn_norm, mem_norm, xattn_w_q, xattn_w_kv, xattn_q_norm, xattn_k_norm, xattn_w_o, ffn2_norm, ffn2_w_gate, ffn2_w_up, ffn2_w_down):
    B, S, D = x.shape
    L = w_in.shape[0]
    T = B * S
    M = mem.shape[1]
    bf = lambda w: w.astype(BF16)
    row = lambda a: a.reshape(L, 1, a.shape[-1])

    up_rows, down_rows = D_MODEL // 8, D_FF // 8
    ffn_w = (_cast_layer(ffn1_w_gate, 0, up_rows), _cast_layer(ffn1_w_up, 0, up_rows),
             _cast_layer(ffn1_w_down, 0, down_rows))
    w_in_b = _w_in_layout(w_in)
    w_out_b = bf(w_out)
    wq_b, wkv_b, wo_b = bf(xattn_w_q), bf(xattn_w_kv), bf(xattn_w_o)
    w1k = bf(_dup(cmp_w1_k.reshape(L, CMP_BLOCK, HEAD_DIM, CMP_HIDDEN).swapaxes(2, 3)).swapaxes(2, 3))
    w1v = bf(_dup(cmp_w1_v.reshape(L, CMP_BLOCK, HEAD_DIM, CMP_HIDDEN).swapaxes(2, 3)).swapaxes(2, 3))
    w2k, w2v = bf(_dup(cmp_w2_k)), bf(_dup(cmp_w2_v))
    pek, pev = _dup(cmp_pe_k), _dup(cmp_pe_v)
    qn, kn = row(_dup(q_norm)), _dup(k_norm)
    eneg = _selection_mask_tiles(S)

    f1n, f2n, mixn = row(ffn1_norm), row(ffn2_norm), row(mix_norm)
    ona, onc = row(out_norm_nsa), row(out_norm_conv)
    xan, memn = row(xattn_norm), row(mem_norm)
    xqn, xkn = row(xattn_q_norm), row(xattn_k_norm)

    xs = x.reshape(T, D)
    mem2 = mem.reshape(B * M, D)
    for l in range(L):
        xs, ffn_w = _ffn(xs, f1n, l, *ffn_w, next_weights=(ffn2_w_gate, ffn2_w_up, ffn2_w_down, l))
        proj = _norm_matmul(xs, mixn, w_in_b, l, 1024, PROJ_TN, "mixer_in")
        kc, vc, ks, vs, kw, vw = _prep(proj, pek, w1k, w2k, pev, w1v, w2v, kn, eneg, l, B, S)
        o_a = _nsa(proj, qn, kc, vc, ks, vs, kw, vw, l, B, S)
        kv = _norm_matmul(mem2, memn, wkv_b, l, 512, 1024, "mem_kv")
        xs = _mix_xattn(xs, o_a, proj, conv_w, ona, onc, w_out_b, xan, wq_b, xqn, xkn, kv, wo_b, l, B, S)
        following = (ffn1_w_gate, ffn1_w_up, ffn1_w_down, l + 1) if l + 1 < L else None
        xs, ffn_w = _ffn(xs, f2n, l, *ffn_w, next_weights=following)
    return xs.reshape(B, S, D)
```

```python
import functools
import math

import jax
import jax.numpy as jnp
from jax import lax
from jax.experimental import pallas as pl
from jax.experimental.pallas import tpu as pltpu

F32 = jnp.float32
BF16 = jnp.bfloat16

D_MODEL = 2048
D_FF = 5632
EPS = 1e-6
MASK_VALUE = -1e30
FORCE_SCORE = 1e4
LOG2E = math.log2(math.e)

NSA_HEADS = 16
KV_GROUPS = 2
HEADS_PER_GROUP = NSA_HEADS // KV_GROUPS
HEAD_DIM = 64
NSA_WIDTH = NSA_HEADS * HEAD_DIM
GROUP_WIDTH = NSA_WIDTH // KV_GROUPS
HEAD_PAIRS = GROUP_WIDTH // 128
N_BRANCH = 3
CMP_BLOCK = 32
CMP_STRIDE = 16
CMP_HIDDEN = 4 * HEAD_DIM
SEL_BLOCK = 64
N_SEL = 8
WINDOW = 512
Q_BLOCK = 128
CONV_WIDTH = 1024
CONV_K = 3
XA_HEADS = 4
XA_HEAD_DIM = 128
XA_WIDTH = XA_HEADS * XA_HEAD_DIM

LANES = 128
SUBLANES = 8
KEY_TILE = 128
N_CMP_PAD = 128
CMP_PER_SEL = SEL_BLOCK // CMP_STRIDE
V_ROWS = LANES + 16
L_ROW = LANES

COL_KV = NSA_WIDTH
COL_GATE = COL_KV + 6 * LANES
MAIN_COLS = COL_GATE + 2 * LANES
COL_CONV = MAIN_COLS
CONV_SRC = COL_GATE + N_BRANCH * NSA_HEADS
PROJ_COLS = COL_CONV + 3 * CONV_WIDTH
PROJ_TN = 1024

VMEM_LIMIT = 56 * 1024 * 1024


def _cparams(sem):
    return pltpu.CompilerParams(dimension_semantics=sem, vmem_limit_bytes=VMEM_LIMIT)


def _rms(x, g):
    ms = jnp.mean(x * x, axis=-1, keepdims=True)
    return x * lax.rsqrt(ms + EPS) * g


def _dot(a, b):
    return jnp.dot(a, b, preferred_element_type=F32)


def _dot_nt(a, b):
    return lax.dot_general(a, b, (((1,), (1,)), ((), ())), preferred_element_type=F32)


def _layer_spec(tail_shape, layer, tail_index, single_buffer=False):
    mode = pl.Buffered(1) if single_buffer else None
    return pl.BlockSpec((None,) + tuple(tail_shape), lambda *g: (layer,) + tuple(tail_index(*g)),
                        pipeline_mode=mode)


def _cast_kernel(w_ref, o_ref):
    o_ref[...] = w_ref[...].astype(BF16)


def _cast_layer(w, layer, rows):
    _, R, C = w.shape
    return pl.pallas_call(
        _cast_kernel,
        grid=(R // rows,),
        in_specs=[_layer_spec((rows, C), layer, lambda r: (r, 0))],
        out_specs=pl.BlockSpec((rows, C), lambda r: (r, 0)),
        out_shape=jax.ShapeDtypeStruct((R, C), BF16),
        compiler_params=_cparams(("parallel",)),
        name="cast_bf16",
    )(w)


FFN_SUB = 256
FFN_TM = 512
FFN_TF = 512


def _ffn_kernel(x_ref, g_ref, wg_ref, wu_ref, wd_ref, *rest):
    if len(rest) == 2:
        next_f32, o_ref, next_bf16, h_ref = (), rest[0], (), rest[1]
    else:
        next_f32, o_ref, next_bf16, h_ref = rest[0:3], rest[3], rest[4:7], rest[7]

    @pl.when(pl.program_id(1) == 0)
    def _():
        x = x_ref[...]
        h_ref[...] = _rms(x, g_ref[...]).astype(BF16)
        o_ref[...] = x

    h = h_ref[...]
    part = None
    for c in range(wg_ref.shape[1] // FFN_SUB):
        cols = slice(c * FFN_SUB, (c + 1) * FFN_SUB)
        a = _dot(h, wg_ref[:, cols])
        u = _dot(h, wu_ref[:, cols])
        act = (a * jax.nn.sigmoid(a) * (0.5 * u)).astype(BF16)
        d = _dot(act, wd_ref[cols, :])
        part = d if part is None else part + d
    o_ref[...] += part
    for src, dst in zip(next_f32, next_bf16):
        dst[...] = src[...].astype(BF16)


def _ffn(x, g, layer, wg, wu, wd, next_weights=None):
    T = x.shape[0]
    n_i, n_j = T // FFN_TM, D_FF // FFN_TF
    in_specs = [
        pl.BlockSpec((FFN_TM, D_MODEL), lambda i, j: (i, 0)),
        _layer_spec((1, D_MODEL), layer, lambda i, j: (0, 0)),
        pl.BlockSpec((D_MODEL, FFN_TF), lambda i, j: (0, j)),
        pl.BlockSpec((D_MODEL, FFN_TF), lambda i, j: (0, j)),
        pl.BlockSpec((FFN_TF, D_MODEL), lambda i, j: (j, 0)),
    ]
    out_specs = [pl.BlockSpec((FFN_TM, D_MODEL), lambda i, j: (i, 0))]
    out_shape = [jax.ShapeDtypeStruct((T, D_MODEL), F32)]
    args = [x, g, wg, wu, wd]
    if next_weights is not None:
        ng, nu, nd, nl = next_weights
        up_blk = (D_MODEL // n_i, D_FF // n_j)
        down_blk = (D_FF // n_j, D_MODEL // n_i)
        in_specs += [_layer_spec(up_blk, nl, lambda i, j: (i, j)),
                     _layer_spec(up_blk, nl, lambda i, j: (i, j)),
                     _layer_spec(down_blk, nl, lambda i, j: (j, i))]
        out_specs += [pl.BlockSpec(up_blk, lambda i, j: (i, j)),
                      pl.BlockSpec(up_blk, lambda i, j: (i, j)),
                      pl.BlockSpec(down_blk, lambda i, j: (j, i))]
        out_shape += [jax.ShapeDtypeStruct((D_MODEL, D_FF), BF16), jax.ShapeDtypeStruct((D_MODEL, D_FF), BF16),
                      jax.ShapeDtypeStruct((D_FF, D_MODEL), BF16)]
        args += [ng, nu, nd]
    outs = pl.pallas_call(
        _ffn_kernel,
        grid=(n_i, n_j),
        in_specs=in_specs,
        out_specs=out_specs,
        out_shape=out_shape,
        scratch_shapes=[pltpu.VMEM((FFN_TM, D_MODEL), BF16)],
        compiler_params=_cparams(("parallel", "arbitrary")),
        name="ffn",
    )(*args)
    return outs[0], tuple(outs[1:])


def _norm_matmul_kernel(x_ref, g_ref, w_ref, o_ref, h_ref, *, w_transposed):
    @pl.when(pl.program_id(1) == 0)
    def _():
        h_ref[...] = _rms(x_ref[...], g_ref[...]).astype(BF16)

    o_ref[...] = (_dot_nt if w_transposed else _dot)(h_ref[...], w_ref[...])


def _norm_matmul(x, g, w, layer, tm, tn, name, w_transposed=False):
    T, K = x.shape
    N = w.shape[1] if w_transposed else w.shape[2]
    w_spec = (_layer_spec((tn, K), layer, lambda i, j: (j, 0)) if w_transposed
              else _layer_spec((K, tn), layer, lambda i, j: (0, j)))
    return pl.pallas_call(
        functools.partial(_norm_matmul_kernel, w_transposed=w_transposed),
        grid=(T // tm, N // tn),
        in_specs=[
            pl.BlockSpec((tm, K), lambda i, j: (i, 0)),
            _layer_spec((1, K), layer, lambda i, j: (0, 0)),
            w_spec,
        ],
        out_specs=pl.BlockSpec((tm, tn), lambda i, j: (i, j)),
        out_shape=jax.ShapeDtypeStruct((T, N), F32),
        scratch_shapes=[pltpu.VMEM((tm, K), BF16)],
        compiler_params=_cparams(("parallel", "arbitrary")),
        name=name,
    )(x, g, w)


N_MAIN_TILES = MAIN_COLS // PROJ_TN
CONV_SHIFT = CONV_SRC % PROJ_TN


def _w_in_layout_kernel(a_ref, b_ref, o_ref):
    j = pl.program_id(1)

    @pl.when(j < N_MAIN_TILES)
    def _():
        o_ref[...] = a_ref[...].astype(BF16)

    @pl.when(j >= N_MAIN_TILES)
    def _():
        o_ref[:PROJ_TN - CONV_SHIFT, :] = a_ref[CONV_SHIFT:, :].astype(BF16)
        o_ref[PROJ_TN - CONV_SHIFT:, :] = b_ref[:CONV_SHIFT, :].astype(BF16)


def _w_in_layout(w_in_t):
    L, C, D = w_in_t.shape
    a_blk = lambda j: j - (j >= N_MAIN_TILES).astype(jnp.int32)
    return pl.pallas_call(
        _w_in_layout_kernel,
        grid=(L, PROJ_COLS // PROJ_TN),
        in_specs=[pl.BlockSpec((None, PROJ_TN, D), lambda l, j: (l, a_blk(j), 0)),
                  pl.BlockSpec((None, PROJ_TN, D), lambda l, j: (l, jnp.maximum(a_blk(j) + 1, N_MAIN_TILES), 0))],
        out_specs=pl.BlockSpec((None, PROJ_TN, D), lambda l, j: (l, j, 0)),
        out_shape=jax.ShapeDtypeStruct((L, PROJ_COLS, D), BF16),
        compiler_params=_cparams(("parallel", "parallel")),
        name="w_in_layout",
    )(w_in_t, w_in_t)


def _half_rms(x, lo, gain):
    sq = x * x
    s_lo = jnp.sum(jnp.where(lo, sq, 0.0), axis=-1, keepdims=True)
    s_hi = jnp.sum(jnp.where(lo, 0.0, sq), axis=-1, keepdims=True)
    inv = jnp.where(lo, lax.rsqrt(s_lo * (1.0 / HEAD_DIM) + EPS),
                    lax.rsqrt(s_hi * (1.0 / HEAD_DIM) + EPS))
    return x * inv * gain


def _prep_kernel(ck_ref, cv_ref, s_ref, w_ref, pek_ref, w1k_ref, w2k_ref, pev_ref, w1v_ref, w2v_ref,
                 kn_ref, eneg_ref, kc_o, vc_o, ks_o, vs_o, kw_o, vw_o):
    S = cv_ref.shape[0]
    n_tiles = S // KEY_TILE
    lo = lax.broadcasted_iota(jnp.int32, (1, LANES), 1) < HEAD_DIM
    half = CMP_BLOCK // 2

    def compress(src_ref, pe_ref, w1_ref, w2_ref):
        acc_a = jnp.zeros((2 * N_CMP_PAD, CMP_HIDDEN), F32)
        acc_b = jnp.zeros((2 * N_CMP_PAD, CMP_HIDDEN), F32)
        for l in range(half):
            x = src_ref[pl.ds(l, N_CMP_PAD, stride=CMP_STRIDE), :]
            for acc_is_b, ll in ((False, l), (True, l + half)):
                xp = x + pe_ref[ll:ll + 1, :]
                x2 = jnp.concatenate([jnp.where(lo, xp, 0.0), jnp.where(lo, 0.0, xp)],
                                     axis=0).astype(BF16)
                d = _dot(x2, w1_ref[ll])
                if acc_is_b:
                    acc_b = acc_b + d
                else:
                    acc_a = acc_a + d
        hidden = acc_a + pltpu.roll(acc_b, 2 * N_CMP_PAD - 1, 0)
        act = jax.nn.gelu(hidden, approximate=True).astype(BF16)
        return _dot(act, w2_ref[...])

    kc = _half_rms(compress(ck_ref, pek_ref, w1k_ref, w2k_ref), lo, kn_ref[0:1, :])
    vc = compress(cv_ref, pev_ref, w1v_ref, w2v_ref)
    top = lax.broadcasted_iota(jnp.int32, (LANES, 1), 0) < HEAD_DIM
    for g in range(KV_GROUPS):
        rows = slice(g * N_CMP_PAD, (g + 1) * N_CMP_PAD)
        kc_o[g, 0:N_CMP_PAD, :] = jnp.where(lo, kc[rows], 0.0).astype(BF16)
        kc_o[g, N_CMP_PAD:, :] = jnp.where(lo, 0.0, kc[rows]).astype(BF16)
        vt = vc[rows].T
        vc_o[g, :, 0:N_CMP_PAD] = jnp.where(top, vt, 0.0).astype(BF16)
        vc_o[g, :, N_CMP_PAD:] = jnp.where(top, 0.0, vt).astype(BF16)

    tail_row = lax.broadcasted_iota(jnp.int32, (V_ROWS - LANES, 2 * KEY_TILE), 0)
    tail_col = lax.broadcasted_iota(jnp.int32, (V_ROWS - LANES, 2 * KEY_TILE), 1)
    v_tail = jnp.where(tail_row == tail_col // KEY_TILE, 1.0, 0.0).astype(BF16)

    def emit(src_ref, gain, k_o, v_o, with_mask):
        k = _half_rms(src_ref[:, 0:LANES], lo, gain)
        k_sw = pltpu.roll(k, HEAD_DIM, 1)
        vt = src_ref[:, LANES:2 * LANES].T.astype(BF16)
        zeros = jnp.zeros((HEAD_DIM, KEY_TILE), BF16)
        for g in range(KV_GROUPS):
            k_lo, k_hi = (k, k_sw) if g == 0 else (k_sw, k)
            shape3 = (n_tiles, KEY_TILE, LANES)
            k_o[g, :, 0:KEY_TILE, 0:LANES] = jnp.where(lo, k_lo, 0.0).astype(BF16).reshape(shape3)
            k_o[g, :, KEY_TILE:, 0:LANES] = jnp.where(lo, 0.0, k_hi).astype(BF16).reshape(shape3)
            if with_mask:
                k_o[g, :, 0:KEY_TILE, LANES:] = eneg_ref[...]
                k_o[g, :, KEY_TILE:, LANES:] = eneg_ref[...]
            for kt in range(n_tiles):
                blk = vt[g * HEAD_DIM:(g + 1) * HEAD_DIM, kt * KEY_TILE:(kt + 1) * KEY_TILE]
                v_o[g, kt, 0:HEAD_DIM, 0:KEY_TILE] = blk
                v_o[g, kt, 0:HEAD_DIM, KEY_TILE:] = zeros
                v_o[g, kt, HEAD_DIM:LANES, 0:KEY_TILE] = zeros
                v_o[g, kt, HEAD_DIM:LANES, KEY_TILE:] = blk
                v_o[g, kt, LANES:, :] = v_tail

    emit(s_ref, kn_ref[1:2, :], ks_o, vs_o, True)
    emit(w_ref, kn_ref[2:3, :], kw_o, vw_o, False)


def _prep(proj, pek, w1k, w2k, pev, w1v, w2v, kn, eneg, layer, B, S):
    n_tiles = S // KEY_TILE
    kv_blk = COL_KV // 256
    full = lambda a: _layer_spec(a.shape[1:], layer, lambda b: (0,) * (a.ndim - 1))
    per_batch = lambda *tail: (jax.ShapeDtypeStruct((B, KV_GROUPS) + tail, BF16),
                               pl.BlockSpec((None, KV_GROUPS) + tail, lambda b: (b,) + (0,) * (len(tail) + 1)))
    kc_shape, kc_spec = per_batch(2 * N_CMP_PAD, LANES)
    vc_shape, vc_spec = per_batch(LANES, 2 * N_CMP_PAD)
    ks_shape, ks_spec = per_batch(n_tiles, 2 * KEY_TILE, 2 * LANES)
    kw_shape, kw_spec = per_batch(n_tiles, 2 * KEY_TILE, LANES)
    v_shape, v_spec = per_batch(n_tiles, V_ROWS, 2 * KEY_TILE)
    return pl.pallas_call(
        _prep_kernel,
        grid=(B,),
        in_specs=[
            pl.BlockSpec((S, LANES), lambda b: (b, COL_KV // LANES)),
            pl.BlockSpec((S, LANES), lambda b: (b, COL_KV // LANES + 1)),
            pl.BlockSpec((S, 256), lambda b: (b, kv_blk + 1)),
            pl.BlockSpec((S, 256), lambda b: (b, kv_blk + 2)),
            full(pek), full(w1k), full(w2k), full(pev), full(w1v), full(w2v), full(kn),
            pl.BlockSpec(eneg.shape, lambda b: (0, 0, 0)),
        ],
        out_specs=[kc_spec, vc_spec, ks_spec, v_spec, kw_spec, v_spec],
        out_shape=[kc_shape, vc_shape, ks_shape, v_shape, kw_shape, v_shape],
        compiler_params=_cparams(("parallel",)),
        name="kv_prep",
    )(proj, proj, proj, proj, pek, w1k, w2k, pev, w1v, w2v, kn, eneg)


IMP_PAD = SUBLANES
SEL_CHUNK = 2


def _nsa_group(g, i, q_ref, gt_ref, qn_ref, kc_ref, vc_ref, ks_ref, vs_ref, kw_ref, vw_ref,
               o_ref, qs_ref, pc_ref, pt_ref, og_ref, s_ref, p_ref, acc_ref, st_ref):
    col0 = g * GROUP_WIDTH
    lane = lax.broadcasted_iota(jnp.int32, (Q_BLOCK, LANES), 1)
    row = lax.broadcasted_iota(jnp.int32, (Q_BLOCK, LANES), 0)
    lo = lane < HEAD_DIM
    t_q = i * Q_BLOCK + lane
    cols_all = HEAD_PAIRS * Q_BLOCK

    def tile_cols(a):
        return jnp.concatenate([a] * HEAD_PAIRS, axis=1)

    scale = HEAD_DIM ** -0.5 * LOG2E
    for c in range(HEAD_PAIRS):
        x = q_ref[:, col0 + c * LANES:col0 + (c + 1) * LANES]
        qs_ref[c * Q_BLOCK:(c + 1) * Q_BLOCK, 0:LANES] = (_half_rms(x, lo, qn_ref[...]) * scale).astype(BF16)

    sc = _dot_nt(kc_ref[...], qs_ref[:, 0:LANES])
    valid_c = row * CMP_STRIDE + (CMP_BLOCK - 1) <= t_q
    psum = jnp.zeros((N_CMP_PAD, Q_BLOCK), F32)
    for c in range(HEAD_PAIRS):
        for hf in range(2):
            s = sc[hf * N_CMP_PAD:(hf + 1) * N_CMP_PAD, c * Q_BLOCK:(c + 1) * Q_BLOCK]
            s = jnp.where(valid_c, s, MASK_VALUE)
            e = jnp.where(valid_c, jnp.exp2(s - jnp.max(s, axis=0, keepdims=True)), 0.0)
            den = jnp.sum(e, axis=0, keepdims=True)
            p = e / jnp.where(den > 0.0, den, 1.0)
            psum = psum + p
            pc_ref[hf * N_CMP_PAD:(hf + 1) * N_CMP_PAD, c * Q_BLOCK:(c + 1) * Q_BLOCK] = p.astype(BF16)
    o_cmp = _dot(vc_ref[...], pc_ref[...])

    rows2 = 2 * KEY_TILE
    tail_row = lax.broadcasted_iota(jnp.int32, (V_ROWS - LANES, cols_all), 0)

    def per_head_half(vals, even, odd):
        parts = [vals[:HEAD_DIM] * even, vals[HEAD_DIM:LANES] * odd]
        if vals.shape[0] > LANES:
            parts.append(vals[LANES:] * jnp.where(tail_row == 0, even, jnp.where(tail_row == 1, odd, 0.0)))
        return jnp.concatenate(parts, axis=0)

    def normalised(acc):
        return per_head_half(acc[:LANES], 1.0 / acc[L_ROW:L_ROW + 1], 1.0 / acc[L_ROW + 1:L_ROW + 2])

    n_win = WINDOW // KEY_TILE + 1
    s_win, v_idx = [], []
    for c in range(n_win):
        kt = i - (n_win - 1) + c
        idx = jnp.maximum(kt, 0)
        key = kt * KEY_TILE + row
        lag = t_q - key
        bias = jnp.where((key >= 0) & (lag >= 0) & (lag < WINDOW), 0.0, MASK_VALUE)
        s_win.append(_dot_nt(kw_ref[idx], qs_ref[:, 0:LANES]) + jnp.concatenate([tile_cols(bias)] * 2, axis=0))
        v_idx.append(idx)
    p_w = [[None, None] for _ in range(n_win)]
    for hf in range(2):
        half = slice(hf * KEY_TILE, (hf + 1) * KEY_TILE)
        m = jnp.max(s_win[0][half], axis=0, keepdims=True)
        for c in range(1, n_win):
            m = jnp.maximum(m, jnp.max(s_win[c][half], axis=0, keepdims=True))
        for c in range(n_win):
            p_w[c][hf] = jnp.exp2(s_win[c][half] - m).astype(BF16)
    pv = None
    for c in range(n_win):
        d = _dot(vw_ref[v_idx[c]], jnp.concatenate(p_w[c], axis=0))
        pv = d if pv is None else pv + d
    o_win = normalised(pv)

    def gate_pair(br, c):
        r = br * NSA_HEADS + g * HEADS_PER_GROUP + 2 * c
        return jnp.concatenate([jnp.broadcast_to(gt_ref[r:r + 1, :], (HEAD_DIM, Q_BLOCK)),
                                jnp.broadcast_to(gt_ref[r + 1:r + 2, :], (HEAD_DIM, Q_BLOCK))], axis=0)

    for c in range(HEAD_PAIRS):
        cols = slice(c * Q_BLOCK, (c + 1) * Q_BLOCK)
        og_ref[:, cols] = gate_pair(0, c) * o_cmp[:, cols] + gate_pair(2, c) * o_win[:, cols]

    n_blk = N_CMP_PAD // CMP_PER_SEL
    pt_ref[0:IMP_PAD, :] = jnp.zeros((IMP_PAD, LANES), F32)
    pt_ref[IMP_PAD:, :] = psum
    imp = pt_ref[pl.ds(IMP_PAD - 1, n_blk, stride=CMP_PER_SEL), :]
    for d in range(CMP_PER_SEL):
        imp = imp + pt_ref[pl.ds(IMP_PAD + d, n_blk, stride=CMP_PER_SEL), :]
    blk = lax.broadcasted_iota(jnp.int32, (n_blk, LANES), 0)
    cur = (i * Q_BLOCK + lax.broadcasted_iota(jnp.int32, (n_blk, LANES), 1)) // SEL_BLOCK
    forced = (blk == 0) | (blk == cur) | (blk == cur - 1)
    score = jnp.where(blk > cur, -1.0, jnp.where(forced, FORCE_SCORE, imp))
    rank = jnp.zeros((n_blk, LANES), F32)
    for j in range(n_blk):
        other = jnp.broadcast_to(score[j:j + 1, :], (n_blk, LANES))
        beats = (other > score) | ((other == score) & (blk > j))
        rank = rank + jnp.where(beats, 1.0, 0.0)
    not_sel = jnp.where(rank < float(N_SEL), 0.0, 1.0)
    not_sel = jnp.concatenate([not_sel, jnp.zeros((LANES - n_blk, LANES), F32)], axis=0)
    not_sel_q = not_sel.T.astype(BF16)
    for c in range(HEAD_PAIRS):
        qs_ref[c * Q_BLOCK:(c + 1) * Q_BLOCK, LANES:] = not_sel_q

    M_E, M_O, A_E, A_O = (SUBLANES * r for r in range(4))
    st_row = lambda r: st_ref[r:r + 1, :]
    last_chunk = i // SEL_CHUNK

    def chunk_scores(r):
        for c in range(SEL_CHUNK):
            s_ref[c * rows2:(c + 1) * rows2, :] = _dot_nt(ks_ref[r * SEL_CHUNK + c], qs_ref[...])

    def chunk_pv(r):
        pv = None
        for c in range(SEL_CHUNK):
            d = _dot(vs_ref[r * SEL_CHUNK + c], p_ref[c * rows2:(c + 1) * rows2, :])
            pv = d if pv is None else pv + d
        return per_head_half(acc_ref[...], st_row(A_E), st_row(A_O)) + pv

    def chunk_softmax(causal_chunk):
        bias = None
        if causal_chunk is not None:
            bias = [tile_cols(jnp.where((causal_chunk * SEL_CHUNK + c) * KEY_TILE + row <= t_q, 0.0, MASK_VALUE))
                    for c in range(SEL_CHUNK)]
        p_new = [[None, None] for _ in range(SEL_CHUNK)]
        for hf, (m_r, a_r) in enumerate(((M_E, A_E), (M_O, A_O))):
            s = [s_ref[c * rows2 + hf * KEY_TILE:c * rows2 + (hf + 1) * KEY_TILE, :] for c in range(SEL_CHUNK)]
            if bias is not None:
                s = [s[c] + bias[c] for c in range(SEL_CHUNK)]
            m_prev = st_row(m_r)
            m_new = m_prev
            for c in range(SEL_CHUNK):
                m_new = jnp.maximum(m_new, jnp.max(s[c], axis=0, keepdims=True))
            for c in range(SEL_CHUNK):
                p_new[c][hf] = jnp.exp2(s[c] - m_new).astype(BF16)
            st_ref[m_r:m_r + 1, :] = m_new
            st_ref[a_r:a_r + 1, :] = jnp.exp2(m_prev - m_new)
        return [jnp.concatenate(p_new[c], axis=0) for c in range(SEL_CHUNK)]

    def store_p(p_new):
        for c in range(SEL_CHUNK):
            p_ref[c * rows2:(c + 1) * rows2, :] = p_new[c]

    chunk_scores(0)
    acc_ref[...] = jnp.zeros(acc_ref.shape, F32)
    p_ref[...] = jnp.zeros(p_ref.shape, BF16)
    st_ref[M_E:A_E, :] = jnp.full((2 * SUBLANES, cols_all), MASK_VALUE, F32)
    st_ref[A_E:, :] = jnp.zeros((2 * SUBLANES, cols_all), F32)

    def loop_step(r):
        acc_new = chunk_pv(jnp.maximum(r - 1, 0))
        p_new = chunk_softmax(None)
        acc_ref[...] = acc_new
        store_p(p_new)
        chunk_scores(r + 1)

    def finish():
        acc_new = chunk_pv(jnp.maximum(last_chunk - 1, 0))
        p_new = chunk_softmax(last_chunk)
        acc_ref[...] = acc_new
        store_p(p_new)
        o_sel = normalised(chunk_pv(last_chunk))
        for c in range(HEAD_PAIRS):
            cols = slice(c * Q_BLOCK, (c + 1) * Q_BLOCK)
            o_t = og_ref[:, cols] + gate_pair(1, c) * o_sel[:, cols]
            o_ref[:, col0 + c * LANES:col0 + (c + 1) * LANES] = o_t.T

    return loop_step, finish


def _nsa_kernel(q_ref, gate_ref, qn_ref, kc_ref, vc_ref, ks_ref, vs_ref, kw_ref, vw_ref,
                o_ref, gt_ref, *group_scratch):
    i = pl.program_id(1)
    gt_ref[...] = jax.nn.sigmoid(gate_ref[...]).T
    groups = [_nsa_group(g, i, q_ref, gt_ref, qn_ref, kc_ref.at[g], vc_ref.at[g], ks_ref.at[g], vs_ref.at[g],
                         kw_ref.at[g], vw_ref.at[g], o_ref, *(ref.at[g] for ref in group_scratch))
              for g in range(KV_GROUPS)]

    def body(r, carry):
        for loop_step, _ in groups:
            loop_step(r)
        return carry

    lax.fori_loop(0, i // SEL_CHUNK, body, 0)
    for _, finish in groups:
        finish()


def _nsa(proj, qn, kc, vc, ks, vs, kw, vw, layer, B, S):
    n_q = S // Q_BLOCK
    gate_blk = COL_GATE // LANES
    per_batch = lambda a: pl.BlockSpec((None,) + a.shape[1:], lambda b, i: (b,) + (0,) * (a.ndim - 1))
    cols_all = HEAD_PAIRS * Q_BLOCK
    per_group = lambda shape, dtype: pltpu.VMEM((KV_GROUPS,) + shape, dtype)
    return pl.pallas_call(
        _nsa_kernel,
        grid=(B, n_q),
        in_specs=[
            pl.BlockSpec((Q_BLOCK, NSA_WIDTH), lambda b, i: (b * n_q + i, 0)),
            pl.BlockSpec((Q_BLOCK, LANES), lambda b, i: (b * n_q + i, gate_blk)),
            _layer_spec((1, LANES), layer, lambda b, i: (0, 0)),
            per_batch(kc), per_batch(vc), per_batch(ks), per_batch(vs), per_batch(kw), per_batch(vw),
        ],
        out_specs=pl.BlockSpec((Q_BLOCK, NSA_WIDTH), lambda b, i: (b * n_q + i, 0)),
        out_shape=jax.ShapeDtypeStruct((B * S, NSA_WIDTH), F32),
        scratch_shapes=[
            pltpu.VMEM((LANES, Q_BLOCK), F32),
            per_group((cols_all, 2 * LANES), BF16),
            per_group((2 * N_CMP_PAD, cols_all), BF16),
            per_group((IMP_PAD + N_CMP_PAD, LANES), F32),
            per_group((LANES, cols_all), F32),
            per_group((SEL_CHUNK * 2 * KEY_TILE, cols_all), F32),
            per_group((SEL_CHUNK * 2 * KEY_TILE, cols_all), BF16),
            per_group((V_ROWS, cols_all), F32),
            per_group((4 * SUBLANES, cols_all), F32),
        ],
        compiler_params=_cparams(("parallel", "arbitrary")),
        name="nsa_attention",
    )(proj, proj, qn, kc, vc, ks, vs, kw, vw)


MIX_TM = 512
CARRY_ROWS = SUBLANES


def _mix_xattn_kernel(x_ref, a_ref, b_ref, c_ref, xv_ref, cw_ref, ga_ref, gb_ref, wa_ref, wb_ref,
                      g_ref, wq_ref, qn_ref, kn_ref, kv_ref, wo_ref, o_ref, carry_ref, oh_ref):
    @pl.when(pl.program_id(1) == 0)
    def _():
        carry_ref[...] = jnp.zeros(carry_ref.shape, F32)

    u = c_ref[...] * xv_ref[...]
    tm = u.shape[0]
    row = lax.broadcasted_iota(jnp.int32, u.shape, 0)
    prev1 = carry_ref[CARRY_ROWS - 1:CARRY_ROWS, :]
    prev2 = carry_ref[CARRY_ROWS - 2:CARRY_ROWS - 1, :]
    u1 = jnp.where(row == 0, prev1, pltpu.roll(u, 1, 0))
    u2 = jnp.where(row == 0, prev2, jnp.where(row == 1, prev1, pltpu.roll(u, 2, 0)))
    carry_ref[...] = u[tm - CARRY_ROWS:, :]
    o_b = b_ref[...] * (cw_ref[2:3, :] * u + cw_ref[1:2, :] * u1 + cw_ref[0:1, :] * u2)

    a = _rms(a_ref[...], ga_ref[...]).astype(BF16)
    b = _rms(o_b, gb_ref[...]).astype(BF16)
    x = x_ref[...] + _dot(a, wa_ref[...]) + _dot(b, wb_ref[...])

    h = _rms(x, g_ref[...]).astype(BF16)
    q = _dot(h, wq_ref[...])
    scale = XA_HEAD_DIM ** -0.5
    for hd in range(XA_HEADS):
        cols = slice(hd * XA_HEAD_DIM, (hd + 1) * XA_HEAD_DIM)
        qh = _rms(q[:, cols], qn_ref[...]).astype(BF16)
        kh = _rms(kv_ref[:, cols], kn_ref[...]).astype(BF16)
        vh = kv_ref[:, XA_WIDTH + hd * XA_HEAD_DIM:XA_WIDTH + (hd + 1) * XA_HEAD_DIM].astype(BF16)
        s = _dot_nt(qh, kh) * scale
        e = jnp.exp(s - jnp.max(s, axis=-1, keepdims=True))
        p = (e / jnp.sum(e, axis=-1, keepdims=True)).astype(BF16)
        oh_ref[:, cols] = _dot(p, vh).astype(BF16)
    o_ref[...] = x + _dot(oh_ref[...], wo_ref[...])


def _mix_xattn(x, o_a, proj, conv_w, ga, gb, w_out, g, wq, qn, kn, kv, wo, layer, B, S):
    M = kv.shape[0] // B
    n_t = S // MIX_TM
    conv_blk = COL_CONV // CONV_WIDTH
    tile = lambda width, col: pl.BlockSpec((MIX_TM, width), lambda b, i: (b * n_t + i, col))
    const = lambda shape, idx=(0, 0): _layer_spec(shape, layer, lambda b, i: idx, single_buffer=True)
    return pl.pallas_call(
        _mix_xattn_kernel,
        grid=(B, n_t),
        in_specs=[
            tile(D_MODEL, 0), tile(NSA_WIDTH, 0),
            tile(CONV_WIDTH, conv_blk), tile(CONV_WIDTH, conv_blk + 1), tile(CONV_WIDTH, conv_blk + 2),
            const((CONV_K, CONV_WIDTH)), const((1, NSA_WIDTH)), const((1, CONV_WIDTH)),
            const((NSA_WIDTH, D_MODEL)), const((CONV_WIDTH, D_MODEL), (1, 0)),
            const((1, D_MODEL)), const((D_MODEL, XA_WIDTH)),
            const((1, XA_HEAD_DIM)), const((1, XA_HEAD_DIM)),
            pl.BlockSpec((M, 2 * XA_WIDTH), lambda b, i: (b, 0)),
            const((XA_WIDTH, D_MODEL)),
        ],
        out_specs=tile(D_MODEL, 0),
        out_shape=jax.ShapeDtypeStruct((B * S, D_MODEL), F32),
        scratch_shapes=[pltpu.VMEM((CARRY_ROWS, CONV_WIDTH), F32), pltpu.VMEM((MIX_TM, XA_WIDTH), BF16)],
        compiler_params=_cparams(("parallel", "arbitrary")),
        name="mixer_out_xattn",
    )(x, o_a, proj, proj, proj, conv_w, ga, gb, w_out, w_out, g, wq, qn, kn, kv, wo)


def _dup(v):
    return jnp.concatenate([v, v], axis=-1)


def _selection_mask_tiles(S):
    j = jnp.arange(LANES)[None, None, :]
    k = (jnp.arange(S // KEY_TILE)[:, None, None] * KEY_TILE + jnp.arange(KEY_TILE)[None, :, None])
    return jnp.where(k // SEL_BLOCK == j, MASK_VALUE, 0.0).astype(BF16)


def kernel(x, mem, ffn1_norm, ffn1_w_gate, ffn1_w_up, ffn1_w_down, mix_norm, w_in, cmp_pe_k, cmp_w1_k, cmp_w2_k, cmp_pe_v, cmp_w1_v, cmp_w2_v, q_norm, k_norm, conv_w, out_norm_nsa, out_norm_conv, w_out, xattn_norm, mem_norm, xattn_w_q, xattn_w_kv, xattn_q_norm, xattn_k_norm, xattn_w_o, ffn2_norm, ffn2_w_gate, ffn2_w_up, ffn2_w_down):
    B, S, D = x.shape
    L = w_in.shape[0]
    T = B * S
    M = mem.shape[1]
    bf = lambda w: w.astype(BF16)
    row = lambda a: a.reshape(L, 1, a.shape[-1])

    up_rows, down_rows = D_MODEL // 8, D_FF // 8
    ffn_w = (_cast_layer(ffn1_w_gate, 0, up_rows), _cast_layer(ffn1_w_up, 0, up_rows),
             _cast_layer(ffn1_w_down, 0, down_rows))
    w_in_b = _w_in_layout(jnp.swapaxes(w_in, 1, 2))
    w_out_b = bf(w_out)
    wq_b, wkv_b, wo_b = bf(xattn_w_q), bf(xattn_w_kv), bf(xattn_w_o)
    w1k = bf(_dup(cmp_w1_k.reshape(L, CMP_BLOCK, HEAD_DIM, CMP_HIDDEN).swapaxes(2, 3)).swapaxes(2, 3))
    w1v = bf(_dup(cmp_w1_v.reshape(L, CMP_BLOCK, HEAD_DIM, CMP_HIDDEN).swapaxes(2, 3)).swapaxes(2, 3))
    w2k, w2v = bf(_dup(cmp_w2_k)), bf(_dup(cmp_w2_v))
    pek, pev = _dup(cmp_pe_k), _dup(cmp_pe_v)
    qn, kn = row(_dup(q_norm)), _dup(k_norm)
    eneg = _selection_mask_tiles(S)

    f1n, f2n, mixn = row(ffn1_norm), row(ffn2_norm), row(mix_norm)
    ona, onc = row(out_norm_nsa), row(out_norm_conv)
    xan, memn = row(xattn_norm), row(mem_norm)
    xqn, xkn = row(xattn_q_norm), row(xattn_k_norm)

    xs = x.reshape(T, D)
    mem2 = mem.reshape(B * M, D)
    for l in range(L):
        xs, ffn_w = _ffn(xs, f1n, l, *ffn_w, next_weights=(ffn2_w_gate, ffn2_w_up, ffn2_w_down, l))
        proj = _norm_matmul(xs, mixn, w_in_b, l, 1024, PROJ_TN, "mixer_in", w_transposed=True)
        kc, vc, ks, vs, kw, vw = _prep(proj, pek, w1k, w2k, pev, w1v, w2v, kn, eneg, l, B, S)
        o_a = _nsa(proj, qn, kc, vc, ks, vs, kw, vw, l, B, S)
        kv = _norm_matmul(mem2, memn, wkv_b, l, 512, 1024, "mem_kv")
        xs = _mix_xattn(xs, o_a, proj, conv_w, ona, onc, w_out_b, xan, wq_b, xqn, xkn, kv, wo_b, l, B, S)
        following = (ffn1_w_gate, ffn1_w_up, ffn1_w_down, l + 1) if l + 1 < L else None
        xs, ffn_w = _ffn(xs, f2n, l, *ffn_w, next_weights=following)
    return xs.reshape(B, S, D)
```

```python
import functools
import math

import jax
import jax.numpy as jnp
from jax import lax
from jax.experimental import pallas as pl
from jax.experimental.pallas import tpu as pltpu

F32 = jnp.float32
BF16 = jnp.bfloat16

D_MODEL = 2048
D_FF = 5632
EPS = 1e-6
MASK_VALUE = -1e30
FORCE_SCORE = 1e4
LOG2E = math.log2(math.e)

NSA_HEADS = 16
KV_GROUPS = 2
HEADS_PER_GROUP = NSA_HEADS // KV_GROUPS
HEAD_DIM = 64
NSA_WIDTH = NSA_HEADS * HEAD_DIM
GROUP_WIDTH = NSA_WIDTH // KV_GROUPS
HEAD_PAIRS = GROUP_WIDTH // 128
N_BRANCH = 3
CMP_BLOCK = 32
CMP_STRIDE = 16
CMP_HIDDEN = 4 * HEAD_DIM
SEL_BLOCK = 64
N_SEL = 8
WINDOW = 512
Q_BLOCK = 128
CONV_WIDTH = 1024
CONV_K = 3
XA_HEADS = 4
XA_HEAD_DIM = 128
XA_WIDTH = XA_HEADS * XA_HEAD_DIM

LANES = 128
SUBLANES = 8
KEY_TILE = 128
N_CMP_PAD = 128
CMP_PER_SEL = SEL_BLOCK // CMP_STRIDE
V_ROWS = LANES + 16
L_ROW = LANES

COL_KV = NSA_WIDTH
COL_GATE = COL_KV + 6 * LANES
MAIN_COLS = COL_GATE + 2 * LANES
COL_CONV = MAIN_COLS
CONV_SRC = COL_GATE + N_BRANCH * NSA_HEADS
PROJ_COLS = COL_CONV + 3 * CONV_WIDTH
PROJ_TN = 1024

VMEM_LIMIT = 56 * 1024 * 1024


def _cparams(sem):
    return pltpu.CompilerParams(dimension_semantics=sem, vmem_limit_bytes=VMEM_LIMIT)


def _rms(x, g):
    ms = jnp.mean(x * x, axis=-1, keepdims=True)
    return x * lax.rsqrt(ms + EPS) * g


def _dot(a, b):
    return jnp.dot(a, b, preferred_element_type=F32)


def _dot_nt(a, b):
    return lax.dot_general(a, b, (((1,), (1,)), ((), ())), preferred_element_type=F32)


def _layer_spec(tail_shape, layer, tail_index, single_buffer=False):
    mode = pl.Buffered(1) if single_buffer else None
    return pl.BlockSpec((None,) + tuple(tail_shape), lambda *g: (layer,) + tuple(tail_index(*g)),
                        pipeline_mode=mode)


def _cast_kernel(w_ref, o_ref):
    o_ref[...] = w_ref[...].astype(BF16)


def _cast_layer(w, layer, rows):
    _, R, C = w.shape
    return pl.pallas_call(
        _cast_kernel,
        grid=(R // rows,),
        in_specs=[_layer_spec((rows, C), layer, lambda r: (r, 0))],
        out_specs=pl.BlockSpec((rows, C), lambda r: (r, 0)),
        out_shape=jax.ShapeDtypeStruct((R, C), BF16),
        compiler_params=_cparams(("parallel",)),
        name="cast_bf16",
    )(w)


FFN_SUB = 256
FFN_TM = 512
FFN_TF = 512


def _ffn_kernel(x_ref, g_ref, wg_ref, wu_ref, wd_ref, *rest):
    if len(rest) == 2:
        next_f32, o_ref, next_bf16, h_ref = (), rest[0], (), rest[1]
    else:
        next_f32, o_ref, next_bf16, h_ref = rest[0:3], rest[3], rest[4:7], rest[7]

    @pl.when(pl.program_id(1) == 0)
    def _():
        x = x_ref[...]
        h_ref[...] = _rms(x, g_ref[...]).astype(BF16)
        o_ref[...] = x

    h = h_ref[...]
    part = None
    for c in range(wg_ref.shape[1] // FFN_SUB):
        cols = slice(c * FFN_SUB, (c + 1) * FFN_SUB)
        a = _dot(h, wg_ref[:, cols])
        u = _dot(h, wu_ref[:, cols])
        act = (a * jax.nn.sigmoid(a) * (0.5 * u)).astype(BF16)
        d = _dot(act, wd_ref[cols, :])
        part = d if part is None else part + d
    o_ref[...] += part
    for src, dst in zip(next_f32, next_bf16):
        dst[...] = src[...].astype(BF16)


def _ffn(x, g, layer, wg, wu, wd, next_weights=None):
    T = x.shape[0]
    n_i, n_j = T // FFN_TM, D_FF // FFN_TF
    in_specs = [
        pl.BlockSpec((FFN_TM, D_MODEL), lambda i, j: (i, 0)),
        _layer_spec((1, D_MODEL), layer, lambda i, j: (0, 0)),
        pl.BlockSpec((D_MODEL, FFN_TF), lambda i, j: (0, j)),
        pl.BlockSpec((D_MODEL, FFN_TF), lambda i, j: (0, j)),
        pl.BlockSpec((FFN_TF, D_MODEL), lambda i, j: (j, 0)),
    ]
    out_specs = [pl.BlockSpec((FFN_TM, D_MODEL), lambda i, j: (i, 0))]
    out_shape = [jax.ShapeDtypeStruct((T, D_MODEL), F32)]
    args = [x, g, wg, wu, wd]
    if next_weights is not None:
        ng, nu, nd, nl = next_weights
        up_blk = (D_MODEL // n_i, D_FF // n_j)
        down_blk = (D_FF // n_j, D_MODEL // n_i)
        in_specs += [_layer_spec(up_blk, nl, lambda i, j: (i, j)),
                     _layer_spec(up_blk, nl, lambda i, j: (i, j)),
                     _layer_spec(down_blk, nl, lambda i, j: (j, i))]
        out_specs += [pl.BlockSpec(up_blk, lambda i, j: (i, j)),
                      pl.BlockSpec(up_blk, lambda i, j: (i, j)),
                      pl.BlockSpec(down_blk, lambda i, j: (j, i))]
        out_shape += [jax.ShapeDtypeStruct((D_MODEL, D_FF), BF16), jax.ShapeDtypeStruct((D_MODEL, D_FF), BF16),
                      jax.ShapeDtypeStruct((D_FF, D_MODEL), BF16)]
        args += [ng, nu, nd]
    outs = pl.pallas_call(
        _ffn_kernel,
        grid=(n_i, n_j),
        in_specs=in_specs,
        out_specs=out_specs,
        out_shape=out_shape,
        scratch_shapes=[pltpu.VMEM((FFN_TM, D_MODEL), BF16)],
        compiler_params=_cparams(("parallel", "arbitrary")),
        name="ffn",
    )(*args)
    return outs[0], tuple(outs[1:])


def _norm_matmul_kernel(x_ref, g_ref, w_ref, o_ref, h_ref, *, w_transposed):
    @pl.when(pl.program_id(1) == 0)
    def _():
        h_ref[...] = _rms(x_ref[...], g_ref[...]).astype(BF16)

    o_ref[...] = (_dot_nt if w_transposed else _dot)(h_ref[...], w_ref[...])


def _norm_matmul(x, g, w, layer, tm, tn, name, w_transposed=False):
    T, K = x.shape
    N = w.shape[1] if w_transposed else w.shape[2]
    w_spec = (_layer_spec((tn, K), layer, lambda i, j: (j, 0)) if w_transposed
              else _layer_spec((K, tn), layer, lambda i, j: (0, j)))
    return pl.pallas_call(
        functools.partial(_norm_matmul_kernel, w_transposed=w_transposed),
        grid=(T // tm, N // tn),
        in_specs=[
            pl.BlockSpec((tm, K), lambda i, j: (i, 0)),
            _layer_spec((1, K), layer, lambda i, j: (0, 0)),
            w_spec,
        ],
        out_specs=pl.BlockSpec((tm, tn), lambda i, j: (i, j)),
        out_shape=jax.ShapeDtypeStruct((T, N), F32),
        scratch_shapes=[pltpu.VMEM((tm, K), BF16)],
        compiler_params=_cparams(("parallel", "arbitrary")),
        name=name,
    )(x, g, w)


N_MAIN_TILES = MAIN_COLS // PROJ_TN
CONV_SHIFT = CONV_SRC % PROJ_TN


def _w_in_layout_kernel(a_ref, b_ref, o_ref):
    j = pl.program_id(1)

    @pl.when(j < N_MAIN_TILES)
    def _():
        o_ref[...] = a_ref[...].astype(BF16)

    @pl.when(j >= N_MAIN_TILES)
    def _():
        o_ref[:PROJ_TN - CONV_SHIFT, :] = a_ref[CONV_SHIFT:, :].astype(BF16)
        o_ref[PROJ_TN - CONV_SHIFT:, :] = b_ref[:CONV_SHIFT, :].astype(BF16)


def _w_in_layout(w_in_t):
    L, C, D = w_in_t.shape
    a_blk = lambda j: j - (j >= N_MAIN_TILES).astype(jnp.int32)
    return pl.pallas_call(
        _w_in_layout_kernel,
        grid=(L, PROJ_COLS // PROJ_TN),
        in_specs=[pl.BlockSpec((None, PROJ_TN, D), lambda l, j: (l, a_blk(j), 0)),
                  pl.BlockSpec((None, PROJ_TN, D), lambda l, j: (l, jnp.maximum(a_blk(j) + 1, N_MAIN_TILES), 0))],
        out_specs=pl.BlockSpec((None, PROJ_TN, D), lambda l, j: (l, j, 0)),
        out_shape=jax.ShapeDtypeStruct((L, PROJ_COLS, D), BF16),
        compiler_params=_cparams(("parallel", "parallel")),
        name="w_in_layout",
    )(w_in_t, w_in_t)


def _half_rms(x, lo, gain):
    sq = x * x
    s_lo = jnp.sum(jnp.where(lo, sq, 0.0), axis=-1, keepdims=True)
    s_hi = jnp.sum(jnp.where(lo, 0.0, sq), axis=-1, keepdims=True)
    inv = jnp.where(lo, lax.rsqrt(s_lo * (1.0 / HEAD_DIM) + EPS),
                    lax.rsqrt(s_hi * (1.0 / HEAD_DIM) + EPS))
    return x * inv * gain


def _prep_kernel(ck_ref, cv_ref, s_ref, w_ref, pek_ref, w1k_ref, w2k_ref, pev_ref, w1v_ref, w2v_ref,
                 kn_ref, eneg_ref, kc_o, vc_o, ks_o, vs_o, kw_o, vw_o):
    S = cv_ref.shape[0]
    n_tiles = S // KEY_TILE
    lo = lax.broadcasted_iota(jnp.int32, (1, LANES), 1) < HEAD_DIM
    half = CMP_BLOCK // 2

    def compress(src_ref, pe_ref, w1_ref, w2_ref):
        acc_a = jnp.zeros((2 * N_CMP_PAD, CMP_HIDDEN), F32)
        acc_b = jnp.zeros((2 * N_CMP_PAD, CMP_HIDDEN), F32)
        for l in range(half):
            x = src_ref[pl.ds(l, N_CMP_PAD, stride=CMP_STRIDE), :]
            for acc_is_b, ll in ((False, l), (True, l + half)):
                xp = x + pe_ref[ll:ll + 1, :]
                x2 = jnp.concatenate([jnp.where(lo, xp, 0.0), jnp.where(lo, 0.0, xp)],
                                     axis=0).astype(BF16)
                d = _dot(x2, w1_ref[ll])
                if acc_is_b:
                    acc_b = acc_b + d
                else:
                    acc_a = acc_a + d
        hidden = acc_a + pltpu.roll(acc_b, 2 * N_CMP_PAD - 1, 0)
        act = jax.nn.gelu(hidden, approximate=True).astype(BF16)
        return _dot(act, w2_ref[...])

    kc = _half_rms(compress(ck_ref, pek_ref, w1k_ref, w2k_ref), lo, kn_ref[0:1, :])
    vc = compress(cv_ref, pev_ref, w1v_ref, w2v_ref)
    top = lax.broadcasted_iota(jnp.int32, (LANES, 1), 0) < HEAD_DIM
    for g in range(KV_GROUPS):
        rows = slice(g * N_CMP_PAD, (g + 1) * N_CMP_PAD)
        kc_o[g, 0:N_CMP_PAD, :] = jnp.where(lo, kc[rows], 0.0).astype(BF16)
        kc_o[g, N_CMP_PAD:, :] = jnp.where(lo, 0.0, kc[rows]).astype(BF16)
        vt = vc[rows].T
        vc_o[g, :, 0:N_CMP_PAD] = jnp.where(top, vt, 0.0).astype(BF16)
        vc_o[g, :, N_CMP_PAD:] = jnp.where(top, 0.0, vt).astype(BF16)

    tail_row = lax.broadcasted_iota(jnp.int32, (V_ROWS - LANES, 2 * KEY_TILE), 0)
    tail_col = lax.broadcasted_iota(jnp.int32, (V_ROWS - LANES, 2 * KEY_TILE), 1)
    v_tail = jnp.where(tail_row == tail_col // KEY_TILE, 1.0, 0.0).astype(BF16)

    def emit(src_ref, gain, k_o, v_o, with_mask):
        k = _half_rms(src_ref[:, 0:LANES], lo, gain)
        k_sw = pltpu.roll(k, HEAD_DIM, 1)
        vt = src_ref[:, LANES:2 * LANES].T.astype(BF16)
        zeros = jnp.zeros((HEAD_DIM, KEY_TILE), BF16)
        for g in range(KV_GROUPS):
            k_lo, k_hi = (k, k_sw) if g == 0 else (k_sw, k)
            shape3 = (n_tiles, KEY_TILE, LANES)
            k_o[g, :, 0:KEY_TILE, 0:LANES] = jnp.where(lo, k_lo, 0.0).astype(BF16).reshape(shape3)
            k_o[g, :, KEY_TILE:, 0:LANES] = jnp.where(lo, 0.0, k_hi).astype(BF16).reshape(shape3)
            if with_mask:
                k_o[g, :, 0:KEY_TILE, LANES:] = eneg_ref[...]
                k_o[g, :, KEY_TILE:, LANES:] = eneg_ref[...]
            for kt in range(n_tiles):
                blk = vt[g * HEAD_DIM:(g + 1) * HEAD_DIM, kt * KEY_TILE:(kt + 1) * KEY_TILE]
                v_o[g, kt, 0:HEAD_DIM, 0:KEY_TILE] = blk
                v_o[g, kt, 0:HEAD_DIM, KEY_TILE:] = zeros
                v_o[g, kt, HEAD_DIM:LANES, 0:KEY_TILE] = zeros
                v_o[g, kt, HEAD_DIM:LANES, KEY_TILE:] = blk
                v_o[g, kt, LANES:, :] = v_tail

    emit(s_ref, kn_ref[1:2, :], ks_o, vs_o, True)
    emit(w_ref, kn_ref[2:3, :], kw_o, vw_o, False)


def _prep(proj, pek, w1k, w2k, pev, w1v, w2v, kn, eneg, layer, B, S):
    n_tiles = S // KEY_TILE
    kv_blk = COL_KV // 256
    full = lambda a: _layer_spec(a.shape[1:], layer, lambda b: (0,) * (a.ndim - 1))
    per_batch = lambda *tail: (jax.ShapeDtypeStruct((B, KV_GROUPS) + tail, BF16),
                               pl.BlockSpec((None, KV_GROUPS) + tail, lambda b: (b,) + (0,) * (len(tail) + 1)))
    kc_shape, kc_spec = per_batch(2 * N_CMP_PAD, LANES)
    vc_shape, vc_spec = per_batch(LANES, 2 * N_CMP_PAD)
    ks_shape, ks_spec = per_batch(n_tiles, 2 * KEY_TILE, 2 * LANES)
    kw_shape, kw_spec = per_batch(n_tiles, 2 * KEY_TILE, LANES)
    v_shape, v_spec = per_batch(n_tiles, V_ROWS, 2 * KEY_TILE)
    return pl.pallas_call(
        _prep_kernel,
        grid=(B,),
        in_specs=[
            pl.BlockSpec((S, LANES), lambda b: (b, COL_KV // LANES)),
            pl.BlockSpec((S, LANES), lambda b: (b, COL_KV // LANES + 1)),
            pl.BlockSpec((S, 256), lambda b: (b, kv_blk + 1)),
            pl.BlockSpec((S, 256), lambda b: (b, kv_blk + 2)),
            full(pek), full(w1k), full(w2k), full(pev), full(w1v), full(w2v), full(kn),
            pl.BlockSpec(eneg.shape, lambda b: (0, 0, 0)),
        ],
        out_specs=[kc_spec, vc_spec, ks_spec, v_spec, kw_spec, v_spec],
        out_shape=[kc_shape, vc_shape, ks_shape, v_shape, kw_shape, v_shape],
        compiler_params=_cparams(("parallel",)),
        name="kv_prep",
    )(proj, proj, proj, proj, pek, w1k, w2k, pev, w1v, w2v, kn, eneg)


IMP_PAD = SUBLANES
SEL_CHUNK = 2
Q_PER_STEP = SEL_CHUNK


def _nsa_group(g, i, row0, q_ref, gt_ref, qn_ref, kc_ref, vc_ref, ks_ref, vs_ref, kw_ref, vw_ref,
               o_ref, qs_ref, pc_ref, pt_ref, og_ref, s_ref, p_ref, acc_ref, st_ref):
    col0 = g * GROUP_WIDTH
    q_rows = slice(row0, row0 + Q_BLOCK)
    lane = lax.broadcasted_iota(jnp.int32, (Q_BLOCK, LANES), 1)
    row = lax.broadcasted_iota(jnp.int32, (Q_BLOCK, LANES), 0)
    lo = lane < HEAD_DIM
    t_q = i * Q_BLOCK + lane
    cols_all = HEAD_PAIRS * Q_BLOCK

    def tile_cols(a):
        return jnp.concatenate([a] * HEAD_PAIRS, axis=1)

    scale = HEAD_DIM ** -0.5 * LOG2E
    for c in range(HEAD_PAIRS):
        x = q_ref[q_rows, col0 + c * LANES:col0 + (c + 1) * LANES]
        qs_ref[c * Q_BLOCK:(c + 1) * Q_BLOCK, 0:LANES] = (_half_rms(x, lo, qn_ref[...]) * scale).astype(BF16)

    sc = _dot_nt(kc_ref[...], qs_ref[:, 0:LANES])
    valid_c = row * CMP_STRIDE + (CMP_BLOCK - 1) <= t_q
    psum = jnp.zeros((N_CMP_PAD, Q_BLOCK), F32)
    for c in range(HEAD_PAIRS):
        for hf in range(2):
            s = sc[hf * N_CMP_PAD:(hf + 1) * N_CMP_PAD, c * Q_BLOCK:(c + 1) * Q_BLOCK]
            s = jnp.where(valid_c, s, MASK_VALUE)
            e = jnp.where(valid_c, jnp.exp2(s - jnp.max(s, axis=0, keepdims=True)), 0.0)
            den = jnp.sum(e, axis=0, keepdims=True)
            p = e / jnp.where(den > 0.0, den, 1.0)
            psum = psum + p
            pc_ref[hf * N_CMP_PAD:(hf + 1) * N_CMP_PAD, c * Q_BLOCK:(c + 1) * Q_BLOCK] = p.astype(BF16)
    o_cmp = _dot(vc_ref[...], pc_ref[...])

    rows2 = 2 * KEY_TILE
    tail_row = lax.broadcasted_iota(jnp.int32, (V_ROWS - LANES, cols_all), 0)

    def per_head_half(vals, even, odd):
        parts = [vals[:HEAD_DIM] * even, vals[HEAD_DIM:LANES] * odd]
        if vals.shape[0] > LANES:
            parts.append(vals[LANES:] * jnp.where(tail_row == 0, even, jnp.where(tail_row == 1, odd, 0.0)))
        return jnp.concatenate(parts, axis=0)

    def normalised(acc):
        return per_head_half(acc[:LANES], 1.0 / acc[L_ROW:L_ROW + 1], 1.0 / acc[L_ROW + 1:L_ROW + 2])

    n_win = WINDOW // KEY_TILE + 1
    s_win, v_idx = [], []
    for c in range(n_win):
        kt = i - (n_win - 1) + c
        idx = jnp.maximum(kt, 0)
        key = kt * KEY_TILE + row
        lag = t_q - key
        bias = jnp.where((key >= 0) & (lag >= 0) & (lag < WINDOW), 0.0, MASK_VALUE)
        s_win.append(_dot_nt(kw_ref[idx], qs_ref[:, 0:LANES]) + jnp.concatenate([tile_cols(bias)] * 2, axis=0))
        v_idx.append(idx)
    p_w = [[None, None] for _ in range(n_win)]
    for hf in range(2):
        half = slice(hf * KEY_TILE, (hf + 1) * KEY_TILE)
        m = jnp.max(s_win[0][half], axis=0, keepdims=True)
        for c in range(1, n_win):
            m = jnp.maximum(m, jnp.max(s_win[c][half], axis=0, keepdims=True))
        for c in range(n_win):
            p_w[c][hf] = jnp.exp2(s_win[c][half] - m).astype(BF16)
    pv = None
    for c in range(n_win):
        d = _dot(vw_ref[v_idx[c]], jnp.concatenate(p_w[c], axis=0))
        pv = d if pv is None else pv + d
    o_win = normalised(pv)

    def gate_pair(br, c):
        r = br * NSA_HEADS + g * HEADS_PER_GROUP + 2 * c
        return jnp.concatenate([jnp.broadcast_to(gt_ref[r:r + 1, :], (HEAD_DIM, Q_BLOCK)),
                                jnp.broadcast_to(gt_ref[r + 1:r + 2, :], (HEAD_DIM, Q_BLOCK))], axis=0)

    for c in range(HEAD_PAIRS):
        cols = slice(c * Q_BLOCK, (c + 1) * Q_BLOCK)
        og_ref[:, cols] = gate_pair(0, c) * o_cmp[:, cols] + gate_pair(2, c) * o_win[:, cols]

    n_blk = N_CMP_PAD // CMP_PER_SEL
    pt_ref[0:IMP_PAD, :] = jnp.zeros((IMP_PAD, LANES), F32)
    pt_ref[IMP_PAD:, :] = psum
    imp = pt_ref[pl.ds(IMP_PAD - 1, n_blk, stride=CMP_PER_SEL), :]
    for d in range(CMP_PER_SEL):
        imp = imp + pt_ref[pl.ds(IMP_PAD + d, n_blk, stride=CMP_PER_SEL), :]
    blk = lax.broadcasted_iota(jnp.int32, (n_blk, LANES), 0)
    cur = (i * Q_BLOCK + lax.broadcasted_iota(jnp.int32, (n_blk, LANES), 1)) // SEL_BLOCK
    forced = (blk == 0) | (blk == cur) | (blk == cur - 1)
    score = jnp.where(blk > cur, -1.0, jnp.where(forced, FORCE_SCORE, imp))
    rank = jnp.zeros((n_blk, LANES), F32)
    for j in range(n_blk):
        other = jnp.broadcast_to(score[j:j + 1, :], (n_blk, LANES))
        beats = (other > score) | ((other == score) & (blk > j))
        rank = rank + jnp.where(beats, 1.0, 0.0)
    not_sel = jnp.where(rank < float(N_SEL), 0.0, 1.0)
    not_sel = jnp.concatenate([not_sel, jnp.zeros((LANES - n_blk, LANES), F32)], axis=0)
    not_sel_q = not_sel.T.astype(BF16)
    for c in range(HEAD_PAIRS):
        qs_ref[c * Q_BLOCK:(c + 1) * Q_BLOCK, LANES:] = not_sel_q

    M_E, M_O, A_E, A_O = (SUBLANES * r for r in range(4))
    st_row = lambda r: st_ref[r:r + 1, :]
    last_chunk = i // SEL_CHUNK

    def chunk_scores(r):
        for c in range(SEL_CHUNK):
            s_ref[c * rows2:(c + 1) * rows2, :] = _dot_nt(ks_ref[r * SEL_CHUNK + c], qs_ref[...])

    def chunk_pv(r):
        pv = None
        for c in range(SEL_CHUNK):
            d = _dot(vs_ref[r * SEL_CHUNK + c], p_ref[c * rows2:(c + 1) * rows2, :])
            pv = d if pv is None else pv + d
        return per_head_half(acc_ref[...], st_row(A_E), st_row(A_O)) + pv

    def chunk_softmax(causal_chunk):
        bias = None
        if causal_chunk is not None:
            bias = [tile_cols(jnp.where((causal_chunk * SEL_CHUNK + c) * KEY_TILE + row <= t_q, 0.0, MASK_VALUE))
                    for c in range(SEL_CHUNK)]
        p_new = [[None, None] for _ in range(SEL_CHUNK)]
        for hf, (m_r, a_r) in enumerate(((M_E, A_E), (M_O, A_O))):
            s = [s_ref[c * rows2 + hf * KEY_TILE:c * rows2 + (hf + 1) * KEY_TILE, :] for c in range(SEL_CHUNK)]
            if bias is not None:
                s = [s[c] + bias[c] for c in range(SEL_CHUNK)]
            m_prev = st_row(m_r)
            m_new = m_prev
            for c in range(SEL_CHUNK):
                m_new = jnp.maximum(m_new, jnp.max(s[c], axis=0, keepdims=True))
            for c in range(SEL_CHUNK):
                p_new[c][hf] = jnp.exp2(s[c] - m_new).astype(BF16)
            st_ref[m_r:m_r + 1, :] = m_new
            st_ref[a_r:a_r + 1, :] = jnp.exp2(m_prev - m_new)
        return [jnp.concatenate(p_new[c], axis=0) for c in range(SEL_CHUNK)]

    def store_p(p_new):
        for c in range(SEL_CHUNK):
            p_ref[c * rows2:(c + 1) * rows2, :] = p_new[c]

    chunk_scores(0)
    acc_ref[...] = jnp.zeros(acc_ref.shape, F32)
    p_ref[...] = jnp.zeros(p_ref.shape, BF16)
    st_ref[M_E:A_E, :] = jnp.full((2 * SUBLANES, cols_all), MASK_VALUE, F32)
    st_ref[A_E:, :] = jnp.zeros((2 * SUBLANES, cols_all), F32)

    def loop_step(r):
        acc_new = chunk_pv(jnp.maximum(r - 1, 0))
        p_new = chunk_softmax(None)
        acc_ref[...] = acc_new
        store_p(p_new)
        chunk_scores(r + 1)

    def finish():
        acc_new = chunk_pv(jnp.maximum(last_chunk - 1, 0))
        p_new = chunk_softmax(last_chunk)
        acc_ref[...] = acc_new
        store_p(p_new)
        o_sel = normalised(chunk_pv(last_chunk))
        for c in range(HEAD_PAIRS):
            cols = slice(c * Q_BLOCK, (c + 1) * Q_BLOCK)
            o_t = og_ref[:, cols] + gate_pair(1, c) * o_sel[:, cols]
            o_ref[q_rows, col0 + c * LANES:col0 + (c + 1) * LANES] = o_t.T

    return loop_step, finish


def _nsa_kernel(q_ref, gate_ref, qn_ref, kc_ref, vc_ref, ks_ref, vs_ref, kw_ref, vw_ref,
                o_ref, gt_ref, *group_scratch):
    step = pl.program_id(1)
    groups = []
    for qb in range(Q_PER_STEP):
        rows = slice(qb * Q_BLOCK, (qb + 1) * Q_BLOCK)
        gt_ref[qb] = jax.nn.sigmoid(gate_ref[rows, :]).T
        for g in range(KV_GROUPS):
            scratch = (ref.at[qb * KV_GROUPS + g] for ref in group_scratch)
            groups.append(_nsa_group(g, step * Q_PER_STEP + qb, qb * Q_BLOCK, q_ref, gt_ref.at[qb], qn_ref,
                                     kc_ref.at[g], vc_ref.at[g], ks_ref.at[g], vs_ref.at[g],
                                     kw_ref.at[g], vw_ref.at[g], o_ref, *scratch))

    def body(r, carry):
        for loop_step, _ in groups:
            loop_step(r)
        return carry

    lax.fori_loop(0, step, body, 0)
    for _, finish in groups:
        finish()


def _nsa(proj, qn, kc, vc, ks, vs, kw, vw, layer, B, S):
    step_rows = Q_BLOCK * Q_PER_STEP
    n_steps = S // step_rows
    gate_blk = COL_GATE // LANES
    per_batch = lambda a: pl.BlockSpec((None,) + a.shape[1:], lambda b, i: (b,) + (0,) * (a.ndim - 1))
    cols_all = HEAD_PAIRS * Q_BLOCK
    per_chain = lambda shape, dtype: pltpu.VMEM((Q_PER_STEP * KV_GROUPS,) + shape, dtype)
    return pl.pallas_call(
        _nsa_kernel,
        grid=(B, n_steps),
        in_specs=[
            pl.BlockSpec((step_rows, NSA_WIDTH), lambda b, i: (b * n_steps + i, 0)),
            pl.BlockSpec((step_rows, LANES), lambda b, i: (b * n_steps + i, gate_blk)),
            _layer_spec((1, LANES), layer, lambda b, i: (0, 0)),
            per_batch(kc), per_batch(vc), per_batch(ks), per_batch(vs), per_batch(kw), per_batch(vw),
        ],
        out_specs=pl.BlockSpec((step_rows, NSA_WIDTH), lambda b, i: (b * n_steps + i, 0)),
        out_shape=jax.ShapeDtypeStruct((B * S, NSA_WIDTH), F32),
        scratch_shapes=[
            pltpu.VMEM((Q_PER_STEP, LANES, Q_BLOCK), F32),
            per_chain((cols_all, 2 * LANES), BF16),
            per_chain((2 * N_CMP_PAD, cols_all), BF16),
            per_chain((IMP_PAD + N_CMP_PAD, LANES), F32),
            per_chain((LANES, cols_all), F32),
            per_chain((SEL_CHUNK * 2 * KEY_TILE, cols_all), F32),
            per_chain((SEL_CHUNK * 2 * KEY_TILE, cols_all), BF16),
            per_chain((V_ROWS, cols_all), F32),
            per_chain((4 * SUBLANES, cols_all), F32),
        ],
        compiler_params=_cparams(("parallel", "arbitrary")),
        name="nsa_attention",
    )(proj, proj, qn, kc, vc, ks, vs, kw, vw)


MIX_TM = 512
CARRY_ROWS = SUBLANES


def _mix_xattn_kernel(x_ref, a_ref, b_ref, c_ref, xv_ref, cw_ref, ga_ref, gb_ref, wa_ref, wb_ref,
                      g_ref, wq_ref, qn_ref, kn_ref, kv_ref, wo_ref, o_ref, carry_ref, oh_ref):
    @pl.when(pl.program_id(1) == 0)
    def _():
        carry_ref[...] = jnp.zeros(carry_ref.shape, F32)

    u = c_ref[...] * xv_ref[...]
    tm = u.shape[0]
    row = lax.broadcasted_iota(jnp.int32, u.shape, 0)
    prev1 = carry_ref[CARRY_ROWS - 1:CARRY_ROWS, :]
    prev2 = carry_ref[CARRY_ROWS - 2:CARRY_ROWS - 1, :]
    u1 = jnp.where(row == 0, prev1, pltpu.roll(u, 1, 0))
    u2 = jnp.where(row == 0, prev2, jnp.where(row == 1, prev1, pltpu.roll(u, 2, 0)))
    carry_ref[...] = u[tm - CARRY_ROWS:, :]
    o_b = b_ref[...] * (cw_ref[2:3, :] * u + cw_ref[1:2, :] * u1 + cw_ref[0:1, :] * u2)

    a = _rms(a_ref[...], ga_ref[...]).astype(BF16)
    b = _rms(o_b, gb_ref[...]).astype(BF16)
    x = x_ref[...] + _dot(a, wa_ref[...]) + _dot(b, wb_ref[...])

    h = _rms(x, g_ref[...]).astype(BF16)
    q = _dot(h, wq_ref[...])
    scale = XA_HEAD_DIM ** -0.5
    for hd in range(XA_HEADS):
        cols = slice(hd * XA_HEAD_DIM, (hd + 1) * XA_HEAD_DIM)
        qh = _rms(q[:, cols], qn_ref[...]).astype(BF16)
        kh = _rms(kv_ref[:, cols], kn_ref[...]).astype(BF16)
        vh = kv_ref[:, XA_WIDTH + hd * XA_HEAD_DIM:XA_WIDTH + (hd + 1) * XA_HEAD_DIM].astype(BF16)
        s = _dot_nt(qh, kh) * scale
        e = jnp.exp(s - jnp.max(s, axis=-1, keepdims=True))
        p = (e / jnp.sum(e, axis=-1, keepdims=True)).astype(BF16)
        oh_ref[:, cols] = _dot(p, vh).astype(BF16)
    o_ref[...] = x + _dot(oh_ref[...], wo_ref[...])


def _mix_xattn(x, o_a, proj, conv_w, ga, gb, w_out, g, wq, qn, kn, kv, wo, layer, B, S):
    M = kv.shape[0] // B
    n_t = S // MIX_TM
    conv_blk = COL_CONV // CONV_WIDTH
    tile = lambda width, col: pl.BlockSpec((MIX_TM, width), lambda b, i: (b * n_t + i, col))
    const = lambda shape, idx=(0, 0): _layer_spec(shape, layer, lambda b, i: idx, single_buffer=True)
    return pl.pallas_call(
        _mix_xattn_kernel,
        grid=(B, n_t),
        in_specs=[
            tile(D_MODEL, 0), tile(NSA_WIDTH, 0),
            tile(CONV_WIDTH, conv_blk), tile(CONV_WIDTH, conv_blk + 1), tile(CONV_WIDTH, conv_blk + 2),
            const((CONV_K, CONV_WIDTH)), const((1, NSA_WIDTH)), const((1, CONV_WIDTH)),
            const((NSA_WIDTH, D_MODEL)), const((CONV_WIDTH, D_MODEL), (1, 0)),
            const((1, D_MODEL)), const((D_MODEL, XA_WIDTH)),
            const((1, XA_HEAD_DIM)), const((1, XA_HEAD_DIM)),
            pl.BlockSpec((M, 2 * XA_WIDTH), lambda b, i: (b, 0)),
            const((XA_WIDTH, D_MODEL)),
        ],
        out_specs=tile(D_MODEL, 0),
        out_shape=jax.ShapeDtypeStruct((B * S, D_MODEL), F32),
        scratch_shapes=[pltpu.VMEM((CARRY_ROWS, CONV_WIDTH), F32), pltpu.VMEM((MIX_TM, XA_WIDTH), BF16)],
        compiler_params=_cparams(("parallel", "arbitrary")),
        name="mixer_out_xattn",
    )(x, o_a, proj, proj, proj, conv_w, ga, gb, w_out, w_out, g, wq, qn, kn, kv, wo)


def _dup(v):
    return jnp.concatenate([v, v], axis=-1)


def _selection_mask_tiles(S):
    j = jnp.arange(LANES)[None, None, :]
    k = (jnp.arange(S // KEY_TILE)[:, None, None] * KEY_TILE + jnp.arange(KEY_TILE)[None, :, None])
    return jnp.where(k // SEL_BLOCK == j, MASK_VALUE, 0.0).astype(BF16)


def kernel(x, mem, ffn1_norm, ffn1_w_gate, ffn1_w_up, ffn1_w_down, mix_norm, w_in, cmp_pe_k, cmp_w1_k, cmp_w2_k, cmp_pe_v, cmp_w1_v, cmp_w2_v, q_norm, k_norm, conv_w, out_norm_nsa, out_norm_conv, w_out, xattn_norm, mem_norm, xattn_w_q, xattn_w_kv, xattn_q_norm, xattn_k_norm, xattn_w_o, ffn2_norm, ffn2_w_gate, ffn2_w_up, ffn2_w_down):
    B, S, D = x.shape
    L = w_in.shape[0]
    T = B * S
    M = mem.shape[1]
    bf = lambda w: w.astype(BF16)
    row = lambda a: a.reshape(L, 1, a.shape[-1])

    up_rows, down_rows = D_MODEL // 8, D_FF // 8
    ffn_w = (_cast_layer(ffn1_w_gate, 0, up_rows), _cast_layer(ffn1_w_up, 0, up_rows),
             _cast_layer(ffn1_w_down, 0, down_rows))
    w_in_b = _w_in_layout(jnp.swapaxes(w_in, 1, 2))
    w_out_b = bf(w_out)
    wq_b, wkv_b, wo_b = bf(xattn_w_q), bf(xattn_w_kv), bf(xattn_w_o)
    w1k = bf(_dup(cmp_w1_k.reshape(L, CMP_BLOCK, HEAD_DIM, CMP_HIDDEN).swapaxes(2, 3)).swapaxes(2, 3))
    w1v = bf(_dup(cmp_w1_v.reshape(L, CMP_BLOCK, HEAD_DIM, CMP_HIDDEN).swapaxes(2, 3)).swapaxes(2, 3))
    w2k, w2v = bf(_dup(cmp_w2_k)), bf(_dup(cmp_w2_v))
    pek, pev = _dup(cmp_pe_k), _dup(cmp_pe_v)
    qn, kn = row(_dup(q_norm)), _dup(k_norm)
    eneg = _selection_mask_tiles(S)

    f1n, f2n, mixn = row(ffn1_norm), row(ffn2_norm), row(mix_norm)
    ona, onc = row(out_norm_nsa), row(out_norm_conv)
    xan, memn = row(xattn_norm), row(mem_norm)
    xqn, xkn = row(xattn_q_norm), row(xattn_k_norm)

    xs = x.reshape(T, D)
    mem2 = mem.reshape(B * M, D)
    for l in range(L):
        xs, ffn_w = _ffn(xs, f1n, l, *ffn_w, next_weights=(ffn2_w_gate, ffn2_w_up, ffn2_w_down, l))
        proj = _norm_matmul(xs, mixn, w_in_b, l, 1024, PROJ_TN, "mixer_in", w_transposed=True)
        kc, vc, ks, vs, kw, vw = _prep(proj, pek, w1k, w2k, pev, w1v, w2v, kn, eneg, l, B, S)
        o_a = _nsa(proj, qn, kc, vc, ks, vs, kw, vw, l, B, S)
        kv = _norm_matmul(mem2, memn, wkv_b, l, 512, 1024, "mem_kv")
        xs = _mix_xattn(xs, o_a, proj, conv_w, ona, onc, w_out_b, xan, wq_b, xqn, xkn, kv, wo_b, l, B, S)
        following = (ffn1_w_gate, ffn1_w_up, ffn1_w_down, l + 1) if l + 1 < L else None
        xs, ffn_w = _ffn(xs, f2n, l, *ffn_w, next_weights=following)
    return xs.reshape(B, S, D)
```

```python
import functools
import math

import jax
import jax.numpy as jnp
from jax import lax
from jax.experimental import pallas as pl
from jax.experimental.pallas import tpu as pltpu

F32 = jnp.float32
BF16 = jnp.bfloat16

D_MODEL = 2048
D_FF = 5632
EPS = 1e-6
MASK_VALUE = -1e30
FORCE_SCORE = 1e4
LOG2E = math.log2(math.e)

LANES = 128
SUBLANES = 8

NSA_HEADS = 16
KV_GROUPS = 2
HEADS_PER_GROUP = NSA_HEADS // KV_GROUPS
HEAD_DIM = 64
NSA_WIDTH = NSA_HEADS * HEAD_DIM
GROUP_WIDTH = NSA_WIDTH // KV_GROUPS
HEAD_PAIRS = GROUP_WIDTH // LANES
N_BRANCH = 3
CMP_BLOCK = 32
CMP_STRIDE = 16
CMP_HIDDEN = 4 * HEAD_DIM
SEL_BLOCK = 64
N_SEL = 8
WINDOW = 512
Q_BLOCK = 128
CONV_WIDTH = 1024
CONV_K = 3
XA_HEADS = 4
XA_HEAD_DIM = 128
XA_WIDTH = XA_HEADS * XA_HEAD_DIM

KEY_TILE = 128
N_CMP_PAD = 128
CMP_PER_SEL = SEL_BLOCK // CMP_STRIDE
V_ROWS = LANES + 16
L_ROW = LANES

COL_KV = NSA_WIDTH
COL_GATE = COL_KV + 6 * LANES
MAIN_COLS = COL_GATE + 2 * LANES
COL_CONV = MAIN_COLS
CONV_SRC = COL_GATE + N_BRANCH * NSA_HEADS
PROJ_COLS = COL_CONV + 3 * CONV_WIDTH
PROJ_TN = 1024

V7X_VMEM_BYTES = 64 * 1024 * 1024
VMEM_LIMIT = V7X_VMEM_BYTES * 7 // 8


def _cparams(sem):
    return pltpu.CompilerParams(dimension_semantics=sem, vmem_limit_bytes=VMEM_LIMIT)


def _rms(x, g):
    ms = jnp.mean(x * x, axis=-1, keepdims=True)
    return x * lax.rsqrt(ms + EPS) * g


def _dot(a, b):
    return jnp.dot(a, b, preferred_element_type=F32)


def _dot_nt(a, b):
    return lax.dot_general(a, b, (((1,), (1,)), ((), ())), preferred_element_type=F32)


def _layer_spec(tail_shape, layer, tail_index, single_buffer=False):
    mode = pl.Buffered(1) if single_buffer else None
    return pl.BlockSpec((None,) + tuple(tail_shape), lambda *g: (layer,) + tuple(tail_index(*g)),
                        pipeline_mode=mode)


def _cast_kernel(w_ref, o_ref):
    o_ref[...] = w_ref[...].astype(BF16)


def _cast_layer(w, layer, rows):
    _, R, C = w.shape
    return pl.pallas_call(
        _cast_kernel,
        grid=(R // rows,),
        in_specs=[_layer_spec((rows, C), layer, lambda r: (r, 0))],
        out_specs=pl.BlockSpec((rows, C), lambda r: (r, 0)),
        out_shape=jax.ShapeDtypeStruct((R, C), BF16),
        compiler_params=_cparams(("parallel",)),
        name="cast_bf16",
    )(w)


FFN_SUB = 256
FFN_TM = 512
FFN_TF = 512


def _ffn_kernel(x_ref, g_ref, wg_ref, wu_ref, wd_ref, *rest):
    if len(rest) == 2:
        next_f32, o_ref, next_bf16, h_ref = (), rest[0], (), rest[1]
    else:
        next_f32, o_ref, next_bf16, h_ref = rest[0:3], rest[3], rest[4:7], rest[7]

    @pl.when(pl.program_id(1) == 0)
    def _():
        x = x_ref[...]
        h_ref[...] = _rms(x, g_ref[...]).astype(BF16)
        o_ref[...] = x

    h = h_ref[...]
    part = None
    for c in range(wg_ref.shape[1] // FFN_SUB):
        cols = slice(c * FFN_SUB, (c + 1) * FFN_SUB)
        a = _dot(h, wg_ref[:, cols])
        u = _dot(h, wu_ref[:, cols])
        act = (a * jax.nn.sigmoid(a) * (0.5 * u)).astype(BF16)
        d = _dot(act, wd_ref[cols, :])
        part = d if part is None else part + d
    o_ref[...] += part
    for src, dst in zip(next_f32, next_bf16):
        dst[...] = src[...].astype(BF16)


def _ffn(x, g, layer, wg, wu, wd, next_weights=None):
    T = x.shape[0]
    n_i, n_j = T // FFN_TM, D_FF // FFN_TF
    in_specs = [
        pl.BlockSpec((FFN_TM, D_MODEL), lambda i, j: (i, 0)),
        _layer_spec((1, D_MODEL), layer, lambda i, j: (0, 0)),
        pl.BlockSpec((D_MODEL, FFN_TF), lambda i, j: (0, j)),
        pl.BlockSpec((D_MODEL, FFN_TF), lambda i, j: (0, j)),
        pl.BlockSpec((FFN_TF, D_MODEL), lambda i, j: (j, 0)),
    ]
    out_specs = [pl.BlockSpec((FFN_TM, D_MODEL), lambda i, j: (i, 0))]
    out_shape = [jax.ShapeDtypeStruct((T, D_MODEL), F32)]
    args = [x, g, wg, wu, wd]
    if next_weights is not None:
        ng, nu, nd, nl = next_weights
        up_blk = (D_MODEL // n_i, D_FF // n_j)
        down_blk = (D_FF // n_j, D_MODEL // n_i)
        in_specs += [_layer_spec(up_blk, nl, lambda i, j: (i, j)),
                     _layer_spec(up_blk, nl, lambda i, j: (i, j)),
                     _layer_spec(down_blk, nl, lambda i, j: (j, i))]
        out_specs += [pl.BlockSpec(up_blk, lambda i, j: (i, j)),
                      pl.BlockSpec(up_blk, lambda i, j: (i, j)),
                      pl.BlockSpec(down_blk, lambda i, j: (j, i))]
        out_shape += [jax.ShapeDtypeStruct((D_MODEL, D_FF), BF16), jax.ShapeDtypeStruct((D_MODEL, D_FF), BF16),
                      jax.ShapeDtypeStruct((D_FF, D_MODEL), BF16)]
        args += [ng, nu, nd]
    outs = pl.pallas_call(
        _ffn_kernel,
        grid=(n_i, n_j),
        in_specs=in_specs,
        out_specs=out_specs,
        out_shape=out_shape,
        scratch_shapes=[pltpu.VMEM((FFN_TM, D_MODEL), BF16)],
        compiler_params=_cparams(("parallel", "arbitrary")),
        name="ffn",
    )(*args)
    return outs[0], tuple(outs[1:])


def _norm_matmul_kernel(x_ref, g_ref, w_ref, o_ref, h_ref, *, w_transposed):
    @pl.when(pl.program_id(1) == 0)
    def _():
        h_ref[...] = _rms(x_ref[...], g_ref[...]).astype(BF16)

    o_ref[...] = (_dot_nt if w_transposed else _dot)(h_ref[...], w_ref[...])


def _norm_matmul(x, g, w, layer, tm, tn, name, w_transposed=False):
    T, K = x.shape
    N = w.shape[1] if w_transposed else w.shape[2]
    w_spec = (_layer_spec((tn, K), layer, lambda i, j: (j, 0)) if w_transposed
              else _layer_spec((K, tn), layer, lambda i, j: (0, j)))
    return pl.pallas_call(
        functools.partial(_norm_matmul_kernel, w_transposed=w_transposed),
        grid=(T // tm, N // tn),
        in_specs=[
            pl.BlockSpec((tm, K), lambda i, j: (i, 0)),
            _layer_spec((1, K), layer, lambda i, j: (0, 0)),
            w_spec,
        ],
        out_specs=pl.BlockSpec((tm, tn), lambda i, j: (i, j)),
        out_shape=jax.ShapeDtypeStruct((T, N), F32),
        scratch_shapes=[pltpu.VMEM((tm, K), BF16)],
        compiler_params=_cparams(("parallel", "arbitrary")),
        name=name,
    )(x, g, w)


N_MAIN_TILES = MAIN_COLS // PROJ_TN
CONV_SHIFT = CONV_SRC % PROJ_TN


def _w_in_layout_kernel(a_ref, b_ref, o_ref):
    j = pl.program_id(1)

    @pl.when(j < N_MAIN_TILES)
    def _():
        o_ref[...] = a_ref[...].astype(BF16)

    @pl.when(j >= N_MAIN_TILES)
    def _():
        o_ref[:PROJ_TN - CONV_SHIFT, :] = a_ref[CONV_SHIFT:, :].astype(BF16)
        o_ref[PROJ_TN - CONV_SHIFT:, :] = b_ref[:CONV_SHIFT, :].astype(BF16)


def _w_in_layout(w_in_t):
    L, C, D = w_in_t.shape
    a_blk = lambda j: j - (j >= N_MAIN_TILES).astype(jnp.int32)
    return pl.pallas_call(
        _w_in_layout_kernel,
        grid=(L, PROJ_COLS // PROJ_TN),
        in_specs=[pl.BlockSpec((None, PROJ_TN, D), lambda l, j: (l, a_blk(j), 0)),
                  pl.BlockSpec((None, PROJ_TN, D), lambda l, j: (l, jnp.maximum(a_blk(j) + 1, N_MAIN_TILES), 0))],
        out_specs=pl.BlockSpec((None, PROJ_TN, D), lambda l, j: (l, j, 0)),
        out_shape=jax.ShapeDtypeStruct((L, PROJ_COLS, D), BF16),
        compiler_params=_cparams(("parallel", "parallel")),
        name="w_in_layout",
    )(w_in_t, w_in_t)


def _half_rms(x, lo, gain):
    sq = x * x
    s_lo = jnp.sum(jnp.where(lo, sq, 0.0), axis=-1, keepdims=True)
    s_hi = jnp.sum(jnp.where(lo, 0.0, sq), axis=-1, keepdims=True)
    inv = jnp.where(lo, lax.rsqrt(s_lo * (1.0 / HEAD_DIM) + EPS),
                    lax.rsqrt(s_hi * (1.0 / HEAD_DIM) + EPS))
    return x * inv * gain


def _prep_kernel(ck_ref, cv_ref, s_ref, w_ref, pek_ref, w1k_ref, w2k_ref, pev_ref, w1v_ref, w2v_ref,
                 kn_ref, eneg_ref, kc_o, vc_o, ks_o, vs_o, kw_o, vw_o):
    S = cv_ref.shape[0]
    n_tiles = S // KEY_TILE
    lo = lax.broadcasted_iota(jnp.int32, (1, LANES), 1) < HEAD_DIM
    half = CMP_BLOCK // 2

    def compress(src_ref, pe_ref, w1_ref, w2_ref):
        acc_a = jnp.zeros((2 * N_CMP_PAD, CMP_HIDDEN), F32)
        acc_b = jnp.zeros((2 * N_CMP_PAD, CMP_HIDDEN), F32)
        for l in range(half):
            x = src_ref[pl.ds(l, N_CMP_PAD, stride=CMP_STRIDE), :]
            for acc_is_b, ll in ((False, l), (True, l + half)):
                xp = x + pe_ref[ll:ll + 1, :]
                x2 = jnp.concatenate([jnp.where(lo, xp, 0.0), jnp.where(lo, 0.0, xp)],
                                     axis=0).astype(BF16)
                d = _dot(x2, w1_ref[ll])
                if acc_is_b:
                    acc_b = acc_b + d
                else:
                    acc_a = acc_a + d
        hidden = acc_a + pltpu.roll(acc_b, 2 * N_CMP_PAD - 1, 0)
        act = jax.nn.gelu(hidden, approximate=True).astype(BF16)
        return _dot(act, w2_ref[...])

    kc = _half_rms(compress(ck_ref, pek_ref, w1k_ref, w2k_ref), lo, kn_ref[0:1, :])
    vc = compress(cv_ref, pev_ref, w1v_ref, w2v_ref)
    top = lax.broadcasted_iota(jnp.int32, (LANES, 1), 0) < HEAD_DIM
    for g in range(KV_GROUPS):
        rows = slice(g * N_CMP_PAD, (g + 1) * N_CMP_PAD)
        kc_o[g, 0:N_CMP_PAD, :] = jnp.where(lo, kc[rows], 0.0).astype(BF16)
        kc_o[g, N_CMP_PAD:, :] = jnp.where(lo, 0.0, kc[rows]).astype(BF16)
        vt = vc[rows].T
        vc_o[g, :, 0:N_CMP_PAD] = jnp.where(top, vt, 0.0).astype(BF16)
        vc_o[g, :, N_CMP_PAD:] = jnp.where(top, 0.0, vt).astype(BF16)

    tail_row = lax.broadcasted_iota(jnp.int32, (V_ROWS - LANES, 2 * KEY_TILE), 0)
    tail_col = lax.broadcasted_iota(jnp.int32, (V_ROWS - LANES, 2 * KEY_TILE), 1)
    v_tail = jnp.where(tail_row == tail_col // KEY_TILE, 1.0, 0.0).astype(BF16)

    def emit(src_ref, gain, k_o, v_o, with_mask):
        k = _half_rms(src_ref[:, 0:LANES], lo, gain)
        k_sw = pltpu.roll(k, HEAD_DIM, 1)
        vt = src_ref[:, LANES:2 * LANES].T.astype(BF16)
        zeros = jnp.zeros((HEAD_DIM, KEY_TILE), BF16)
        for g in range(KV_GROUPS):
            k_lo, k_hi = (k, k_sw) if g == 0 else (k_sw, k)
            shape3 = (n_tiles, KEY_TILE, LANES)
            k_o[g, :, 0:KEY_TILE, 0:LANES] = jnp.where(lo, k_lo, 0.0).astype(BF16).reshape(shape3)
            k_o[g, :, KEY_TILE:, 0:LANES] = jnp.where(lo, 0.0, k_hi).astype(BF16).reshape(shape3)
            if with_mask:
                k_o[g, :, 0:KEY_TILE, LANES:] = eneg_ref[...]
                k_o[g, :, KEY_TILE:, LANES:] = eneg_ref[...]
            for kt in range(n_tiles):
                blk = vt[g * HEAD_DIM:(g + 1) * HEAD_DIM, kt * KEY_TILE:(kt + 1) * KEY_TILE]
                v_o[g, kt, 0:HEAD_DIM, 0:KEY_TILE] = blk
                v_o[g, kt, 0:HEAD_DIM, KEY_TILE:] = zeros
                v_o[g, kt, HEAD_DIM:LANES, 0:KEY_TILE] = zeros
                v_o[g, kt, HEAD_DIM:LANES, KEY_TILE:] = blk
                v_o[g, kt, LANES:, :] = v_tail

    emit(s_ref, kn_ref[1:2, :], ks_o, vs_o, True)
    emit(w_ref, kn_ref[2:3, :], kw_o, vw_o, False)


def _prep(proj, pek, w1k, w2k, pev, w1v, w2v, kn, eneg, layer, B, S):
    n_tiles = S // KEY_TILE
    kv_blk = COL_KV // 256
    full = lambda a: _layer_spec(a.shape[1:], layer, lambda b: (0,) * (a.ndim - 1))
    per_batch = lambda *tail: (jax.ShapeDtypeStruct((B, KV_GROUPS) + tail, BF16),
                               pl.BlockSpec((None, KV_GROUPS) + tail, lambda b: (b,) + (0,) * (len(tail) + 1)))
    kc_shape, kc_spec = per_batch(2 * N_CMP_PAD, LANES)
    vc_shape, vc_spec = per_batch(LANES, 2 * N_CMP_PAD)
    ks_shape, ks_spec = per_batch(n_tiles, 2 * KEY_TILE, 2 * LANES)
    kw_shape, kw_spec = per_batch(n_tiles, 2 * KEY_TILE, LANES)
    v_shape, v_spec = per_batch(n_tiles, V_ROWS, 2 * KEY_TILE)
    return pl.pallas_call(
        _prep_kernel,
        grid=(B,),
        in_specs=[
            pl.BlockSpec((S, LANES), lambda b: (b, COL_KV // LANES)),
            pl.BlockSpec((S, LANES), lambda b: (b, COL_KV // LANES + 1)),
            pl.BlockSpec((S, 256), lambda b: (b, kv_blk + 1)),
            pl.BlockSpec((S, 256), lambda b: (b, kv_blk + 2)),
            full(pek), full(w1k), full(w2k), full(pev), full(w1v), full(w2v), full(kn),
            pl.BlockSpec(eneg.shape, lambda b: (0, 0, 0)),
        ],
        out_specs=[kc_spec, vc_spec, ks_spec, v_spec, kw_spec, v_spec],
        out_shape=[kc_shape, vc_shape, ks_shape, v_shape, kw_shape, v_shape],
        compiler_params=_cparams(("parallel",)),
        name="kv_prep",
    )(proj, proj, proj, proj, pek, w1k, w2k, pev, w1v, w2v, kn, eneg)


IMP_PAD = SUBLANES
SEL_CHUNK = 2
Q_PER_STEP = SEL_CHUNK


def _nsa_group(g, i, row0, q_ref, gt_ref, qn_ref, kc_ref, vc_ref, ks_ref, vs_ref, kw_ref, vw_ref,
               o_ref, qs_ref, pc_ref, pt_ref, og_ref, s_ref, p_ref, acc_ref, st_ref):
    col0 = g * GROUP_WIDTH
    q_rows = slice(row0, row0 + Q_BLOCK)
    lane = lax.broadcasted_iota(jnp.int32, (Q_BLOCK, LANES), 1)
    row = lax.broadcasted_iota(jnp.int32, (Q_BLOCK, LANES), 0)
    lo = lane < HEAD_DIM
    t_q = i * Q_BLOCK + lane
    cols_all = HEAD_PAIRS * Q_BLOCK

    def tile_cols(a):
        return jnp.concatenate([a] * HEAD_PAIRS, axis=1)

    scale = HEAD_DIM ** -0.5 * LOG2E
    for c in range(HEAD_PAIRS):
        x = q_ref[q_rows, col0 + c * LANES:col0 + (c + 1) * LANES]
        qs_ref[c * Q_BLOCK:(c + 1) * Q_BLOCK, 0:LANES] = (_half_rms(x, lo, qn_ref[...]) * scale).astype(BF16)

    sc = _dot_nt(kc_ref[...], qs_ref[:, 0:LANES])
    valid_c = row * CMP_STRIDE + (CMP_BLOCK - 1) <= t_q
    psum = jnp.zeros((N_CMP_PAD, Q_BLOCK), F32)
    for c in range(HEAD_PAIRS):
        for hf in range(2):
            s = sc[hf * N_CMP_PAD:(hf + 1) * N_CMP_PAD, c * Q_BLOCK:(c + 1) * Q_BLOCK]
            s = jnp.where(valid_c, s, MASK_VALUE)
            e = jnp.where(valid_c, jnp.exp2(s - jnp.max(s, axis=0, keepdims=True)), 0.0)
            den = jnp.sum(e, axis=0, keepdims=True)
            p = e / jnp.where(den > 0.0, den, 1.0)
            psum = psum + p
            pc_ref[hf * N_CMP_PAD:(hf + 1) * N_CMP_PAD, c * Q_BLOCK:(c + 1) * Q_BLOCK] = p.astype(BF16)
    o_cmp = _dot(vc_ref[...], pc_ref[...])

    rows2 = 2 * KEY_TILE
    tail_row = lax.broadcasted_iota(jnp.int32, (V_ROWS - LANES, cols_all), 0)

    def per_head_half(vals, even, odd):
        parts = [vals[:HEAD_DIM] * even, vals[HEAD_DIM:LANES] * odd]
        if vals.shape[0] > LANES:
            parts.append(vals[LANES:] * jnp.where(tail_row == 0, even, jnp.where(tail_row == 1, odd, 0.0)))
        return jnp.concatenate(parts, axis=0)

    def normalised(acc):
        return per_head_half(acc[:LANES], 1.0 / acc[L_ROW:L_ROW + 1], 1.0 / acc[L_ROW + 1:L_ROW + 2])

    n_win = WINDOW // KEY_TILE + 1
    s_win, v_idx = [], []
    for c in range(n_win):
        kt = i - (n_win - 1) + c
        idx = jnp.maximum(kt, 0)
        key = kt * KEY_TILE + row
        lag = t_q - key
        bias = jnp.where((key >= 0) & (lag >= 0) & (lag < WINDOW), 0.0, MASK_VALUE)
        s_win.append(_dot_nt(kw_ref[idx], qs_ref[:, 0:LANES]) + jnp.concatenate([tile_cols(bias)] * 2, axis=0))
        v_idx.append(idx)
    p_w = [[None, None] for _ in range(n_win)]
    for hf in range(2):
        half = slice(hf * KEY_TILE, (hf + 1) * KEY_TILE)
        m = jnp.max(s_win[0][half], axis=0, keepdims=True)
        for c in range(1, n_win):
            m = jnp.maximum(m, jnp.max(s_win[c][half], axis=0, keepdims=True))
        for c in range(n_win):
            p_w[c][hf] = jnp.exp2(s_win[c][half] - m).astype(BF16)
    pv = None
    for c in range(n_win):
        d = _dot(vw_ref[v_idx[c]], jnp.concatenate(p_w[c], axis=0))
        pv = d if pv is None else pv + d
    o_win = normalised(pv)

    def gate_pair(br, c):
        r = br * NSA_HEADS + g * HEADS_PER_GROUP + 2 * c
        return jnp.concatenate([jnp.broadcast_to(gt_ref[r:r + 1, :], (HEAD_DIM, Q_BLOCK)),
                                jnp.broadcast_to(gt_ref[r + 1:r + 2, :], (HEAD_DIM, Q_BLOCK))], axis=0)

    for c in range(HEAD_PAIRS):
        cols = slice(c * Q_BLOCK, (c + 1) * Q_BLOCK)
        og_ref[:, cols] = gate_pair(0, c) * o_cmp[:, cols] + gate_pair(2, c) * o_win[:, cols]

    n_blk = N_CMP_PAD // CMP_PER_SEL
    pt_ref[0:IMP_PAD, :] = jnp.zeros((IMP_PAD, LANES), F32)
    pt_ref[IMP_PAD:, :] = psum
    imp = pt_ref[pl.ds(IMP_PAD - 1, n_blk, stride=CMP_PER_SEL), :]
    for d in range(CMP_PER_SEL):
        imp = imp + pt_ref[pl.ds(IMP_PAD + d, n_blk, stride=CMP_PER_SEL), :]
    blk = lax.broadcasted_iota(jnp.int32, (n_blk, LANES), 0)
    cur = (i * Q_BLOCK + lax.broadcasted_iota(jnp.int32, (n_blk, LANES), 1)) // SEL_BLOCK
    forced = (blk == 0) | (blk == cur) | (blk == cur - 1)
    score = jnp.where(blk > cur, -1.0, jnp.where(forced, FORCE_SCORE, imp))
    rank = jnp.zeros((n_blk, LANES), F32)
    for j in range(n_blk):
        other = jnp.broadcast_to(score[j:j + 1, :], (n_blk, LANES))
        beats = (other > score) | ((other == score) & (blk > j))
        rank = rank + jnp.where(beats, 1.0, 0.0)
    not_sel = jnp.where(rank < float(N_SEL), 0.0, 1.0)
    not_sel = jnp.concatenate([not_sel, jnp.zeros((LANES - n_blk, LANES), F32)], axis=0)
    not_sel_q = not_sel.T.astype(BF16)
    for c in range(HEAD_PAIRS):
        qs_ref[c * Q_BLOCK:(c + 1) * Q_BLOCK, LANES:] = not_sel_q

    M_E, M_O, A_E, A_O = (SUBLANES * r for r in range(4))
    st_row = lambda r: st_ref[r:r + 1, :]
    last_chunk = i // SEL_CHUNK

    def chunk_scores(r):
        for c in range(SEL_CHUNK):
            s_ref[c * rows2:(c + 1) * rows2, :] = _dot_nt(ks_ref[r * SEL_CHUNK + c], qs_ref[...])

    def chunk_pv(r):
        pv = None
        for c in range(SEL_CHUNK):
            d = _dot(vs_ref[r * SEL_CHUNK + c], p_ref[c * rows2:(c + 1) * rows2, :])
            pv = d if pv is None else pv + d
        return per_head_half(acc_ref[...], st_row(A_E), st_row(A_O)) + pv

    def chunk_softmax(causal_chunk):
        bias = None
        if causal_chunk is not None:
            bias = [tile_cols(jnp.where((causal_chunk * SEL_CHUNK + c) * KEY_TILE + row <= t_q, 0.0, MASK_VALUE))
                    for c in range(SEL_CHUNK)]
        p_new = [[None, None] for _ in range(SEL_CHUNK)]
        for hf, (m_r, a_r) in enumerate(((M_E, A_E), (M_O, A_O))):
            s = [s_ref[c * rows2 + hf * KEY_TILE:c * rows2 + (hf + 1) * KEY_TILE, :] for c in range(SEL_CHUNK)]
            if bias is not None:
                s = [s[c] + bias[c] for c in range(SEL_CHUNK)]
            m_prev = st_row(m_r)
            m_new = m_prev
            for c in range(SEL_CHUNK):
                m_new = jnp.maximum(m_new, jnp.max(s[c], axis=0, keepdims=True))
            for c in range(SEL_CHUNK):
                p_new[c][hf] = jnp.exp2(s[c] - m_new).astype(BF16)
            st_ref[m_r:m_r + 1, :] = m_new
            st_ref[a_r:a_r + 1, :] = jnp.exp2(m_prev - m_new)
        return [jnp.concatenate(p_new[c], axis=0) for c in range(SEL_CHUNK)]

    def store_p(p_new):
        for c in range(SEL_CHUNK):
            p_ref[c * rows2:(c + 1) * rows2, :] = p_new[c]

    chunk_scores(0)
    acc_ref[...] = jnp.zeros(acc_ref.shape, F32)
    p_ref[...] = jnp.zeros(p_ref.shape, BF16)
    st_ref[M_E:A_E, :] = jnp.full((2 * SUBLANES, cols_all), MASK_VALUE, F32)
    st_ref[A_E:, :] = jnp.zeros((2 * SUBLANES, cols_all), F32)

    def loop_step(r):
        acc_new = chunk_pv(jnp.maximum(r - 1, 0))
        p_new = chunk_softmax(None)
        acc_ref[...] = acc_new
        store_p(p_new)
        chunk_scores(r + 1)

    def finish():
        acc_new = chunk_pv(jnp.maximum(last_chunk - 1, 0))
        p_new = chunk_softmax(last_chunk)
        acc_ref[...] = acc_new
        store_p(p_new)
        o_sel = normalised(chunk_pv(last_chunk))
        for c in range(HEAD_PAIRS):
            cols = slice(c * Q_BLOCK, (c + 1) * Q_BLOCK)
            o_t = og_ref[:, cols] + gate_pair(1, c) * o_sel[:, cols]
            o_ref[q_rows, col0 + c * LANES:col0 + (c + 1) * LANES] = o_t.T

    return loop_step, finish


def _nsa_kernel(q_ref, gate_ref, qn_ref, kc_ref, vc_ref, ks_ref, vs_ref, kw_ref, vw_ref,
                o_ref, gt_ref, *group_scratch):
    step = pl.program_id(1)
    groups = []
    for qb in range(Q_PER_STEP):
        rows = slice(qb * Q_BLOCK, (qb + 1) * Q_BLOCK)
        gt_ref[qb] = jax.nn.sigmoid(gate_ref[rows, :]).T
        for g in range(KV_GROUPS):
            scratch = (ref.at[qb * KV_GROUPS + g] for ref in group_scratch)
            groups.append(_nsa_group(g, step * Q_PER_STEP + qb, qb * Q_BLOCK, q_ref, gt_ref.at[qb], qn_ref,
                                     kc_ref.at[g], vc_ref.at[g], ks_ref.at[g], vs_ref.at[g],
                                     kw_ref.at[g], vw_ref.at[g], o_ref, *scratch))

    def body(r, carry):
        for loop_step, _ in groups:
            loop_step(r)
        return carry

    lax.fori_loop(0, step, body, 0)
    for _, finish in groups:
        finish()


def _nsa(proj, qn, kc, vc, ks, vs, kw, vw, layer, B, S):
    step_rows = Q_BLOCK * Q_PER_STEP
    n_steps = S // step_rows
    gate_blk = COL_GATE // LANES
    per_batch = lambda a: pl.BlockSpec((None,) + a.shape[1:], lambda b, i: (b,) + (0,) * (a.ndim - 1))
    cols_all = HEAD_PAIRS * Q_BLOCK
    per_chain = lambda shape, dtype: pltpu.VMEM((Q_PER_STEP * KV_GROUPS,) + shape, dtype)
    return pl.pallas_call(
        _nsa_kernel,
        grid=(B, n_steps),
        in_specs=[
            pl.BlockSpec((step_rows, NSA_WIDTH), lambda b, i: (b * n_steps + i, 0)),
            pl.BlockSpec((step_rows, LANES), lambda b, i: (b * n_steps + i, gate_blk)),
            _layer_spec((1, LANES), layer, lambda b, i: (0, 0)),
            per_batch(kc), per_batch(vc), per_batch(ks), per_batch(vs), per_batch(kw), per_batch(vw),
        ],
        out_specs=pl.BlockSpec((step_rows, NSA_WIDTH), lambda b, i: (b * n_steps + i, 0)),
        out_shape=jax.ShapeDtypeStruct((B * S, NSA_WIDTH), F32),
        scratch_shapes=[
            pltpu.VMEM((Q_PER_STEP, LANES, Q_BLOCK), F32),
            per_chain((cols_all, 2 * LANES), BF16),
            per_chain((2 * N_CMP_PAD, cols_all), BF16),
            per_chain((IMP_PAD + N_CMP_PAD, LANES), F32),
            per_chain((LANES, cols_all), F32),
            per_chain((SEL_CHUNK * 2 * KEY_TILE, cols_all), F32),
            per_chain((SEL_CHUNK * 2 * KEY_TILE, cols_all), BF16),
            per_chain((V_ROWS, cols_all), F32),
            per_chain((4 * SUBLANES, cols_all), F32),
        ],
        compiler_params=_cparams(("parallel", "arbitrary")),
        name="nsa_attention",
    )(proj, proj, qn, kc, vc, ks, vs, kw, vw)


MIX_TM = 512
CARRY_ROWS = SUBLANES


def _mix_xattn_kernel(x_ref, a_ref, b_ref, c_ref, xv_ref, cw_ref, ga_ref, gb_ref, wa_ref, wb_ref,
                      g_ref, wq_ref, qn_ref, kn_ref, kv_ref, wo_ref, o_ref, carry_ref, oh_ref):
    @pl.when(pl.program_id(1) == 0)
    def _():
        carry_ref[...] = jnp.zeros(carry_ref.shape, F32)

    u = c_ref[...] * xv_ref[...]
    tm = u.shape[0]
    row = lax.broadcasted_iota(jnp.int32, u.shape, 0)
    prev1 = carry_ref[CARRY_ROWS - 1:CARRY_ROWS, :]
    prev2 = carry_ref[CARRY_ROWS - 2:CARRY_ROWS - 1, :]
    u1 = jnp.where(row == 0, prev1, pltpu.roll(u, 1, 0))
    u2 = jnp.where(row == 0, prev2, jnp.where(row == 1, prev1, pltpu.roll(u, 2, 0)))
    carry_ref[...] = u[tm - CARRY_ROWS:, :]
    o_b = b_ref[...] * (cw_ref[2:3, :] * u + cw_ref[1:2, :] * u1 + cw_ref[0:1, :] * u2)

    a = _rms(a_ref[...], ga_ref[...]).astype(BF16)
    b = _rms(o_b, gb_ref[...]).astype(BF16)
    x = x_ref[...] + _dot(a, wa_ref[...]) + _dot(b, wb_ref[...])

    h = _rms(x, g_ref[...]).astype(BF16)
    q = _dot(h, wq_ref[...])
    scale = XA_HEAD_DIM ** -0.5
    for hd in range(XA_HEADS):
        cols = slice(hd * XA_HEAD_DIM, (hd + 1) * XA_HEAD_DIM)
        qh = _rms(q[:, cols], qn_ref[...]).astype(BF16)
        kh = _rms(kv_ref[:, cols], kn_ref[...]).astype(BF16)
        vh = kv_ref[:, XA_WIDTH + hd * XA_HEAD_DIM:XA_WIDTH + (hd + 1) * XA_HEAD_DIM].astype(BF16)
        s = _dot_nt(qh, kh) * scale
        e = jnp.exp(s - jnp.max(s, axis=-1, keepdims=True))
        p = (e / jnp.sum(e, axis=-1, keepdims=True)).astype(BF16)
        oh_ref[:, cols] = _dot(p, vh).astype(BF16)
    o_ref[...] = x + _dot(oh_ref[...], wo_ref[...])


def _mix_xattn(x, o_a, proj, conv_w, ga, gb, w_out, g, wq, qn, kn, kv, wo, layer, B, S):
    M = kv.shape[0] // B
    n_t = S // MIX_TM
    conv_blk = COL_CONV // CONV_WIDTH
    tile = lambda width, col: pl.BlockSpec((MIX_TM, width), lambda b, i: (b * n_t + i, col))
    const = lambda shape, idx=(0, 0): _layer_spec(shape, layer, lambda b, i: idx, single_buffer=True)
    return pl.pallas_call(
        _mix_xattn_kernel,
        grid=(B, n_t),
        in_specs=[
            tile(D_MODEL, 0), tile(NSA_WIDTH, 0),
            tile(CONV_WIDTH, conv_blk), tile(CONV_WIDTH, conv_blk + 1), tile(CONV_WIDTH, conv_blk + 2),
            const((CONV_K, CONV_WIDTH)), const((1, NSA_WIDTH)), const((1, CONV_WIDTH)),
            const((NSA_WIDTH, D_MODEL)), const((CONV_WIDTH, D_MODEL), (1, 0)),
            const((1, D_MODEL)), const((D_MODEL, XA_WIDTH)),
            const((1, XA_HEAD_DIM)), const((1, XA_HEAD_DIM)),
            pl.BlockSpec((M, 2 * XA_WIDTH), lambda b, i: (b, 0)),
            const((XA_WIDTH, D_MODEL)),
        ],
        out_specs=tile(D_MODEL, 0),
        out_shape=jax.ShapeDtypeStruct((B * S, D_MODEL), F32),
        scratch_shapes=[pltpu.VMEM((CARRY_ROWS, CONV_WIDTH), F32), pltpu.VMEM((MIX_TM, XA_WIDTH), BF16)],
        compiler_params=_cparams(("parallel", "arbitrary")),
        name="mixer_out_xattn",
    )(x, o_a, proj, proj, proj, conv_w, ga, gb, w_out, w_out, g, wq, qn, kn, kv, wo)


def _dup(v):
    return jnp.concatenate([v, v], axis=-1)


def _selection_mask_tiles(S):
    j = jnp.arange(LANES)[None, None, :]
    k = (jnp.arange(S // KEY_TILE)[:, None, None] * KEY_TILE + jnp.arange(KEY_TILE)[None, :, None])
    return jnp.where(k // SEL_BLOCK == j, MASK_VALUE, 0.0).astype(BF16)


def kernel(x, mem, ffn1_norm, ffn1_w_gate, ffn1_w_up, ffn1_w_down, mix_norm, w_in, cmp_pe_k, cmp_w1_k, cmp_w2_k, cmp_pe_v, cmp_w1_v, cmp_w2_v, q_norm, k_norm, conv_w, out_norm_nsa, out_norm_conv, w_out, xattn_norm, mem_norm, xattn_w_q, xattn_w_kv, xattn_q_norm, xattn_k_norm, xattn_w_o, ffn2_norm, ffn2_w_gate, ffn2_w_up, ffn2_w_down):
    B, S, D = x.shape
    L = w_in.shape[0]
    T = B * S
    M = mem.shape[1]
    bf = lambda w: w.astype(BF16)
    row = lambda a: a.reshape(L, 1, a.shape[-1])

    up_rows, down_rows = D_MODEL // 8, D_FF // 8
    ffn_w = (_cast_layer(ffn1_w_gate, 0, up_rows), _cast_layer(ffn1_w_up, 0, up_rows),
             _cast_layer(ffn1_w_down, 0, down_rows))
    w_in_b = _w_in_layout(jnp.swapaxes(w_in, 1, 2))
    w_out_b = bf(w_out)
    wq_b, wkv_b, wo_b = bf(xattn_w_q), bf(xattn_w_kv), bf(xattn_w_o)
    w1k = bf(_dup(cmp_w1_k.reshape(L, CMP_BLOCK, HEAD_DIM, CMP_HIDDEN).swapaxes(2, 3)).swapaxes(2, 3))
    w1v = bf(_dup(cmp_w1_v.reshape(L, CMP_BLOCK, HEAD_DIM, CMP_HIDDEN).swapaxes(2, 3)).swapaxes(2, 3))
    w2k, w2v = bf(_dup(cmp_w2_k)), bf(_dup(cmp_w2_v))
    pek, pev = _dup(cmp_pe_k), _dup(cmp_pe_v)
    qn, kn = row(_dup(q_norm)), _dup(k_norm)
    eneg = _selection_mask_tiles(S)

    f1n, f2n, mixn = row(ffn1_norm), row(ffn2_norm), row(mix_norm)
    ona, onc = row(out_norm_nsa), row(out_norm_conv)
    xan, memn = row(xattn_norm), row(mem_norm)
    xqn, xkn = row(xattn_q_norm), row(xattn_k_norm)

    xs = x.reshape(T, D)
    mem2 = mem.reshape(B * M, D)
    for l in range(L):
        xs, ffn_w = _ffn(xs, f1n, l, *ffn_w, next_weights=(ffn2_w_gate, ffn2_w_up, ffn2_w_down, l))
        proj = _norm_matmul(xs, mixn, w_in_b, l, 1024, PROJ_TN, "mixer_in", w_transposed=True)
        kc, vc, ks, vs, kw, vw = _prep(proj, pek, w1k, w2k, pev, w1v, w2v, kn, eneg, l, B, S)
        o_a = _nsa(proj, qn, kc, vc, ks, vs, kw, vw, l, B, S)
        kv = _norm_matmul(mem2, memn, wkv_b, l, 512, 1024, "mem_kv")
        xs = _mix_xattn(xs, o_a, proj, conv_w, ona, onc, w_out_b, xan, wq_b, xqn, xkn, kv, wo_b, l, B, S)
        following = (ffn1_w_gate, ffn1_w_up, ffn1_w_down, l + 1) if l + 1 < L else None
        xs, ffn_w = _ffn(xs, f2n, l, *ffn_w, next_weights=following)
    return xs.reshape(B, S, D)
```

```python
import functools
import math

import jax
import jax.numpy as jnp
from jax import lax
from jax.experimental import pallas as pl
from jax.experimental.pallas import tpu as pltpu

F32 = jnp.float32
BF16 = jnp.bfloat16

D_MODEL = 2048
D_FF = 5632
EPS = 1e-6
MASK_VALUE = -1e30
FORCE_SCORE = 1e4
LOG2E = math.log2(math.e)

LANES = 128
SUBLANES = 8

NSA_HEADS = 16
KV_GROUPS = 2
HEADS_PER_GROUP = NSA_HEADS // KV_GROUPS
HEAD_DIM = 64
NSA_WIDTH = NSA_HEADS * HEAD_DIM
GROUP_WIDTH = NSA_WIDTH // KV_GROUPS
HEAD_PAIRS = GROUP_WIDTH // LANES
N_BRANCH = 3
CMP_BLOCK = 32
CMP_STRIDE = 16
CMP_HIDDEN = 4 * HEAD_DIM
SEL_BLOCK = 64
N_SEL = 8
WINDOW = 512
Q_BLOCK = 128
CONV_WIDTH = 1024
CONV_K = 3
XA_HEADS = 4
XA_HEAD_DIM = 128
XA_WIDTH = XA_HEADS * XA_HEAD_DIM

KEY_TILE = 128
N_CMP_PAD = 128
CMP_PER_SEL = SEL_BLOCK // CMP_STRIDE
V_ROWS = LANES + 16
L_ROW = LANES

COL_KV = NSA_WIDTH
COL_GATE = COL_KV + 6 * LANES
MAIN_COLS = COL_GATE + 2 * LANES
COL_CONV = MAIN_COLS
CONV_SRC = COL_GATE + N_BRANCH * NSA_HEADS
PROJ_COLS = COL_CONV + 3 * CONV_WIDTH
PROJ_TN = 1024

V7X_VMEM_BYTES = 64 * 1024 * 1024
VMEM_LIMIT = V7X_VMEM_BYTES * 7 // 8


def _cparams(sem):
    return pltpu.CompilerParams(dimension_semantics=sem, vmem_limit_bytes=VMEM_LIMIT)


def _rms(x, g):
    ms = jnp.mean(x * x, axis=-1, keepdims=True)
    return x * lax.rsqrt(ms + EPS) * g


def _dot(a, b):
    return jnp.dot(a, b, preferred_element_type=F32)


def _dot_nt(a, b):
    return lax.dot_general(a, b, (((1,), (1,)), ((), ())), preferred_element_type=F32)


def _layer_spec(tail_shape, layer, tail_index, single_buffer=False):
    mode = pl.Buffered(1) if single_buffer else None
    return pl.BlockSpec((None,) + tuple(tail_shape), lambda *g: (layer,) + tuple(tail_index(*g)),
                        pipeline_mode=mode)


def _cast_kernel(w_ref, o_ref):
    o_ref[...] = w_ref[...].astype(BF16)


def _cast_layer(w, layer, rows):
    _, R, C = w.shape
    return pl.pallas_call(
        _cast_kernel,
        grid=(R // rows,),
        in_specs=[_layer_spec((rows, C), layer, lambda r: (r, 0))],
        out_specs=pl.BlockSpec((rows, C), lambda r: (r, 0)),
        out_shape=jax.ShapeDtypeStruct((R, C), BF16),
        compiler_params=_cparams(("parallel",)),
        name="cast_bf16",
    )(w)


FFN_SUB = 256
FFN_TM = 1024
FFN_TF = 512


def _ffn_kernel(x_ref, g_ref, wg_ref, wu_ref, wd_ref, *rest):
    if len(rest) == 2:
        next_f32, o_ref, next_bf16, h_ref = (), rest[0], (), rest[1]
    else:
        next_f32, o_ref, next_bf16, h_ref = rest[0:3], rest[3], rest[4:7], rest[7]

    @pl.when(pl.program_id(1) == 0)
    def _():
        x = x_ref[...]
        h_ref[...] = _rms(x, g_ref[...]).astype(BF16)
        o_ref[...] = x

    h = h_ref[...]
    for c in range(wg_ref.shape[1] // FFN_SUB):
        cols = slice(c * FFN_SUB, (c + 1) * FFN_SUB)
        a = _dot(h, wg_ref[:, cols])
        u = _dot(h, wu_ref[:, cols])
        act = (a * jax.nn.sigmoid(a) * (0.5 * u)).astype(BF16)
        o_ref[...] += _dot(act, wd_ref[cols, :])
    for src, dst in zip(next_f32, next_bf16):
        dst[...] = src[...].astype(BF16)


def _ffn(x, g, layer, wg, wu, wd, next_weights=None):
    T = x.shape[0]
    n_i, n_j = T // FFN_TM, D_FF // FFN_TF
    in_specs = [
        pl.BlockSpec((FFN_TM, D_MODEL), lambda i, j: (i, 0), pipeline_mode=pl.Buffered(1)),
        _layer_spec((1, D_MODEL), layer, lambda i, j: (0, 0)),
        pl.BlockSpec((D_MODEL, FFN_TF), lambda i, j: (0, j)),
        pl.BlockSpec((D_MODEL, FFN_TF), lambda i, j: (0, j)),
        pl.BlockSpec((FFN_TF, D_MODEL), lambda i, j: (j, 0)),
    ]
    out_specs = [pl.BlockSpec((FFN_TM, D_MODEL), lambda i, j: (i, 0))]
    out_shape = [jax.ShapeDtypeStruct((T, D_MODEL), F32)]
    args = [x, g, wg, wu, wd]
    if next_weights is not None:
        ng, nu, nd, nl = next_weights
        up_blk = (D_MODEL // n_i, D_FF // n_j)
        down_blk = (D_FF // n_j, D_MODEL // n_i)
        in_specs += [_layer_spec(up_blk, nl, lambda i, j: (i, j)),
                     _layer_spec(up_blk, nl, lambda i, j: (i, j)),
                     _layer_spec(down_blk, nl, lambda i, j: (j, i))]
        out_specs += [pl.BlockSpec(up_blk, lambda i, j: (i, j)),
                      pl.BlockSpec(up_blk, lambda i, j: (i, j)),
                      pl.BlockSpec(down_blk, lambda i, j: (j, i))]
        out_shape += [jax.ShapeDtypeStruct((D_MODEL, D_FF), BF16), jax.ShapeDtypeStruct((D_MODEL, D_FF), BF16),
                      jax.ShapeDtypeStruct((D_FF, D_MODEL), BF16)]
        args += [ng, nu, nd]
    outs = pl.pallas_call(
        _ffn_kernel,
        grid=(n_i, n_j),
        in_specs=in_specs,
        out_specs=out_specs,
        out_shape=out_shape,
        scratch_shapes=[pltpu.VMEM((FFN_TM, D_MODEL), BF16)],
        compiler_params=_cparams(("parallel", "arbitrary")),
        name="ffn",
    )(*args)
    return outs[0], tuple(outs[1:])


def _norm_matmul_kernel(x_ref, g_ref, w_ref, o_ref, h_ref, *, w_transposed):
    @pl.when(pl.program_id(1) == 0)
    def _():
        h_ref[...] = _rms(x_ref[...], g_ref[...]).astype(BF16)

    o_ref[...] = (_dot_nt if w_transposed else _dot)(h_ref[...], w_ref[...])


def _norm_matmul(x, g, w, layer, tm, tn, name, w_transposed=False):
    T, K = x.shape
    N = w.shape[1] if w_transposed else w.shape[2]
    w_spec = (_layer_spec((tn, K), layer, lambda i, j: (j, 0)) if w_transposed
              else _layer_spec((K, tn), layer, lambda i, j: (0, j)))
    return pl.pallas_call(
        functools.partial(_norm_matmul_kernel, w_transposed=w_transposed),
        grid=(T // tm, N // tn),
        in_specs=[
            pl.BlockSpec((tm, K), lambda i, j: (i, 0)),
            _layer_spec((1, K), layer, lambda i, j: (0, 0)),
            w_spec,
        ],
        out_specs=pl.BlockSpec((tm, tn), lambda i, j: (i, j)),
        out_shape=jax.ShapeDtypeStruct((T, N), F32),
        scratch_shapes=[pltpu.VMEM((tm, K), BF16)],
        compiler_params=_cparams(("parallel", "arbitrary")),
        name=name,
    )(x, g, w)


N_MAIN_TILES = MAIN_COLS // PROJ_TN
CONV_SHIFT = CONV_SRC % PROJ_TN


def _w_in_layout_kernel(a_ref, b_ref, o_ref):
    j = pl.program_id(1)

    @pl.when(j < N_MAIN_TILES)
    def _():
        o_ref[...] = a_ref[...].astype(BF16)

    @pl.when(j >= N_MAIN_TILES)
    def _():
        o_ref[:PROJ_TN - CONV_SHIFT, :] = a_ref[CONV_SHIFT:, :].astype(BF16)
        o_ref[PROJ_TN - CONV_SHIFT:, :] = b_ref[:CONV_SHIFT, :].astype(BF16)


def _w_in_layout(w_in_t):
    L, C, D = w_in_t.shape
    a_blk = lambda j: j - (j >= N_MAIN_TILES).astype(jnp.int32)
    return pl.pallas_call(
        _w_in_layout_kernel,
        grid=(L, PROJ_COLS // PROJ_TN),
        in_specs=[pl.BlockSpec((None, PROJ_TN, D), lambda l, j: (l, a_blk(j), 0)),
                  pl.BlockSpec((None, PROJ_TN, D), lambda l, j: (l, jnp.maximum(a_blk(j) + 1, N_MAIN_TILES), 0))],
        out_specs=pl.BlockSpec((None, PROJ_TN, D), lambda l, j: (l, j, 0)),
        out_shape=jax.ShapeDtypeStruct((L, PROJ_COLS, D), BF16),
        compiler_params=_cparams(("parallel", "parallel")),
        name="w_in_layout",
    )(w_in_t, w_in_t)


def _half_rms(x, lo, gain):
    sq = x * x
    s_lo = jnp.sum(jnp.where(lo, sq, 0.0), axis=-1, keepdims=True)
    s_hi = jnp.sum(jnp.where(lo, 0.0, sq), axis=-1, keepdims=True)
    inv = jnp.where(lo, lax.rsqrt(s_lo * (1.0 / HEAD_DIM) + EPS),
                    lax.rsqrt(s_hi * (1.0 / HEAD_DIM) + EPS))
    return x * inv * gain


def _prep_kernel(ck_ref, cv_ref, s_ref, w_ref, pek_ref, w1k_ref, w2k_ref, pev_ref, w1v_ref, w2v_ref,
                 kn_ref, eneg_ref, kc_o, vc_o, ks_o, vs_o, kw_o, vw_o):
    S = cv_ref.shape[0]
    n_tiles = S // KEY_TILE
    lo = lax.broadcasted_iota(jnp.int32, (1, LANES), 1) < HEAD_DIM
    half = CMP_BLOCK // 2

    def compress(src_ref, pe_ref, w1_ref, w2_ref):
        acc_a = jnp.zeros((2 * N_CMP_PAD, CMP_HIDDEN), F32)
        acc_b = jnp.zeros((2 * N_CMP_PAD, CMP_HIDDEN), F32)
        for l in range(half):
            x = src_ref[pl.ds(l, N_CMP_PAD, stride=CMP_STRIDE), :]
            for acc_is_b, ll in ((False, l), (True, l + half)):
                xp = x + pe_ref[ll:ll + 1, :]
                x2 = jnp.concatenate([jnp.where(lo, xp, 0.0), jnp.where(lo, 0.0, xp)],
                                     axis=0).astype(BF16)
                d = _dot(x2, w1_ref[ll])
                if acc_is_b:
                    acc_b = acc_b + d
                else:
                    acc_a = acc_a + d
        hidden = acc_a + pltpu.roll(acc_b, 2 * N_CMP_PAD - 1, 0)
        act = jax.nn.gelu(hidden, approximate=True).astype(BF16)
        return _dot(act, w2_ref[...])

    kc = _half_rms(compress(ck_ref, pek_ref, w1k_ref, w2k_ref), lo, kn_ref[0:1, :])
    vc = compress(cv_ref, pev_ref, w1v_ref, w2v_ref)
    top = lax.broadcasted_iota(jnp.int32, (LANES, 1), 0) < HEAD_DIM
    for g in range(KV_GROUPS):
        rows = slice(g * N_CMP_PAD, (g + 1) * N_CMP_PAD)
        kc_o[g, 0:N_CMP_PAD, :] = jnp.where(lo, kc[rows], 0.0).astype(BF16)
        kc_o[g, N_CMP_PAD:, :] = jnp.where(lo, 0.0, kc[rows]).astype(BF16)
        vt = vc[rows].T
        vc_o[g, :, 0:N_CMP_PAD] = jnp.where(top, vt, 0.0).astype(BF16)
        vc_o[g, :, N_CMP_PAD:] = jnp.where(top, 0.0, vt).astype(BF16)

    tail_row = lax.broadcasted_iota(jnp.int32, (V_ROWS - LANES, 2 * KEY_TILE), 0)
    tail_col = lax.broadcasted_iota(jnp.int32, (V_ROWS - LANES, 2 * KEY_TILE), 1)
    v_tail = jnp.where(tail_row == tail_col // KEY_TILE, 1.0, 0.0).astype(BF16)

    def emit(src_ref, gain, k_o, v_o, with_mask):
        k = _half_rms(src_ref[:, 0:LANES], lo, gain)
        k_sw = pltpu.roll(k, HEAD_DIM, 1)
        vt = src_ref[:, LANES:2 * LANES].T.astype(BF16)
        zeros = jnp.zeros((HEAD_DIM, KEY_TILE), BF16)
        for g in range(KV_GROUPS):
            k_lo, k_hi = (k, k_sw) if g == 0 else (k_sw, k)
            shape3 = (n_tiles, KEY_TILE, LANES)
            k_o[g, :, 0:KEY_TILE, 0:LANES] = jnp.where(lo, k_lo, 0.0).astype(BF16).reshape(shape3)
            k_o[g, :, KEY_TILE:, 0:LANES] = jnp.where(lo, 0.0, k_hi).astype(BF16).reshape(shape3)
            if with_mask:
                k_o[g, :, 0:KEY_TILE, LANES:] = eneg_ref[...]
                k_o[g, :, KEY_TILE:, LANES:] = eneg_ref[...]
            for kt in range(n_tiles):
                blk = vt[g * HEAD_DIM:(g + 1) * HEAD_DIM, kt * KEY_TILE:(kt + 1) * KEY_TILE]
                v_o[g, kt, 0:HEAD_DIM, 0:KEY_TILE] = blk
                v_o[g, kt, 0:HEAD_DIM, KEY_TILE:] = zeros
                v_o[g, kt, HEAD_DIM:LANES, 0:KEY_TILE] = zeros
                v_o[g, kt, HEAD_DIM:LANES, KEY_TILE:] = blk
                v_o[g, kt, LANES:, :] = v_tail

    emit(s_ref, kn_ref[1:2, :], ks_o, vs_o, True)
    emit(w_ref, kn_ref[2:3, :], kw_o, vw_o, False)


def _prep(proj, pek, w1k, w2k, pev, w1v, w2v, kn, eneg, layer, B, S):
    n_tiles = S // KEY_TILE
    kv_blk = COL_KV // 256
    full = lambda a: _layer_spec(a.shape[1:], layer, lambda b: (0,) * (a.ndim - 1))
    per_batch = lambda *tail: (jax.ShapeDtypeStruct((B, KV_GROUPS) + tail, BF16),
                               pl.BlockSpec((None, KV_GROUPS) + tail, lambda b: (b,) + (0,) * (len(tail) + 1)))
    kc_shape, kc_spec = per_batch(2 * N_CMP_PAD, LANES)
    vc_shape, vc_spec = per_batch(LANES, 2 * N_CMP_PAD)
    ks_shape, ks_spec = per_batch(n_tiles, 2 * KEY_TILE, 2 * LANES)
    kw_shape, kw_spec = per_batch(n_tiles, 2 * KEY_TILE, LANES)
    v_shape, v_spec = per_batch(n_tiles, V_ROWS, 2 * KEY_TILE)
    return pl.pallas_call(
        _prep_kernel,
        grid=(B,),
        in_specs=[
            pl.BlockSpec((S, LANES), lambda b: (b, COL_KV // LANES)),
            pl.BlockSpec((S, LANES), lambda b: (b, COL_KV // LANES + 1)),
            pl.BlockSpec((S, 256), lambda b: (b, kv_blk + 1)),
            pl.BlockSpec((S, 256), lambda b: (b, kv_blk + 2)),
            full(pek), full(w1k), full(w2k), full(pev), full(w1v), full(w2v), full(kn),
            pl.BlockSpec(eneg.shape, lambda b: (0, 0, 0)),
        ],
        out_specs=[kc_spec, vc_spec, ks_spec, v_spec, kw_spec, v_spec],
        out_shape=[kc_shape, vc_shape, ks_shape, v_shape, kw_shape, v_shape],
        compiler_params=_cparams(("parallel",)),
        name="kv_prep",
    )(proj, proj, proj, proj, pek, w1k, w2k, pev, w1v, w2v, kn, eneg)


IMP_PAD = SUBLANES
SEL_CHUNK = 2
Q_PER_STEP = SEL_CHUNK


def _nsa_group(g, i, row0, q_ref, gt_ref, qn_ref, kc_ref, vc_ref, ks_ref, vs_ref, kw_ref, vw_ref,
               o_ref, qs_ref, pc_ref, pt_ref, og_ref, s_ref, p_ref, acc_ref, st_ref):
    col0 = g * GROUP_WIDTH
    q_rows = slice(row0, row0 + Q_BLOCK)
    lane = lax.broadcasted_iota(jnp.int32, (Q_BLOCK, LANES), 1)
    row = lax.broadcasted_iota(jnp.int32, (Q_BLOCK, LANES), 0)
    lo = lane < HEAD_DIM
    t_q = i * Q_BLOCK + lane
    cols_all = HEAD_PAIRS * Q_BLOCK

    def tile_cols(a):
        return jnp.concatenate([a] * HEAD_PAIRS, axis=1)

    scale = HEAD_DIM ** -0.5 * LOG2E
    for c in range(HEAD_PAIRS):
        x = q_ref[q_rows, col0 + c * LANES:col0 + (c + 1) * LANES]
        qs_ref[c * Q_BLOCK:(c + 1) * Q_BLOCK, 0:LANES] = (_half_rms(x, lo, qn_ref[...]) * scale).astype(BF16)

    sc = _dot_nt(kc_ref[...], qs_ref[:, 0:LANES])
    valid_c = row * CMP_STRIDE + (CMP_BLOCK - 1) <= t_q
    psum = jnp.zeros((N_CMP_PAD, Q_BLOCK), F32)
    for c in range(HEAD_PAIRS):
        for hf in range(2):
            s = sc[hf * N_CMP_PAD:(hf + 1) * N_CMP_PAD, c * Q_BLOCK:(c + 1) * Q_BLOCK]
            s = jnp.where(valid_c, s, MASK_VALUE)
            e = jnp.where(valid_c, jnp.exp2(s - jnp.max(s, axis=0, keepdims=True)), 0.0)
            den = jnp.sum(e, axis=0, keepdims=True)
            p = e / jnp.where(den > 0.0, den, 1.0)
            psum = psum + p
            pc_ref[hf * N_CMP_PAD:(hf + 1) * N_CMP_PAD, c * Q_BLOCK:(c + 1) * Q_BLOCK] = p.astype(BF16)
    o_cmp = _dot(vc_ref[...], pc_ref[...])

    rows2 = 2 * KEY_TILE
    tail_row = lax.broadcasted_iota(jnp.int32, (V_ROWS - LANES, cols_all), 0)

    def per_head_half(vals, even, odd):
        parts = [vals[:HEAD_DIM] * even, vals[HEAD_DIM:LANES] * odd]
        if vals.shape[0] > LANES:
            parts.append(vals[LANES:] * jnp.where(tail_row == 0, even, jnp.where(tail_row == 1, odd, 0.0)))
        return jnp.concatenate(parts, axis=0)

    def normalised(acc):
        return per_head_half(acc[:LANES], 1.0 / acc[L_ROW:L_ROW + 1], 1.0 / acc[L_ROW + 1:L_ROW + 2])

    n_win = WINDOW // KEY_TILE + 1
    s_win, v_idx = [], []
    for c in range(n_win):
        kt = i - (n_win - 1) + c
        idx = jnp.maximum(kt, 0)
        key = kt * KEY_TILE + row
        lag = t_q - key
        bias = jnp.where((key >= 0) & (lag >= 0) & (lag < WINDOW), 0.0, MASK_VALUE)
        s_win.append(_dot_nt(kw_ref[idx], qs_ref[:, 0:LANES]) + jnp.concatenate([tile_cols(bias)] * 2, axis=0))
        v_idx.append(idx)
    p_w = [[None, None] for _ in range(n_win)]
    for hf in range(2):
        half = slice(hf * KEY_TILE, (hf + 1) * KEY_TILE)
        m = jnp.max(s_win[0][half], axis=0, keepdims=True)
        for c in range(1, n_win):
            m = jnp.maximum(m, jnp.max(s_win[c][half], axis=0, keepdims=True))
        for c in range(n_win):
            p_w[c][hf] = jnp.exp2(s_win[c][half] - m).astype(BF16)
    pv = None
    for c in range(n_win):
        d = _dot(vw_ref[v_idx[c]], jnp.concatenate(p_w[c], axis=0))
        pv = d if pv is None else pv + d
    o_win = normalised(pv)

    def gate_pair(br, c):
        r = br * NSA_HEADS + g * HEADS_PER_GROUP + 2 * c
        return jnp.concatenate([jnp.broadcast_to(gt_ref[r:r + 1, :], (HEAD_DIM, Q_BLOCK)),
                                jnp.broadcast_to(gt_ref[r + 1:r + 2, :], (HEAD_DIM, Q_BLOCK))], axis=0)

    for c in range(HEAD_PAIRS):
        cols = slice(c * Q_BLOCK, (c + 1) * Q_BLOCK)
        og_ref[:, cols] = gate_pair(0, c) * o_cmp[:, cols] + gate_pair(2, c) * o_win[:, cols]

    n_blk = N_CMP_PAD // CMP_PER_SEL
    pt_ref[0:IMP_PAD, :] = jnp.zeros((IMP_PAD, LANES), F32)
    pt_ref[IMP_PAD:, :] = psum
    imp = pt_ref[pl.ds(IMP_PAD - 1, n_blk, stride=CMP_PER_SEL), :]
    for d in range(CMP_PER_SEL):
        imp = imp + pt_ref[pl.ds(IMP_PAD + d, n_blk, stride=CMP_PER_SEL), :]
    blk = lax.broadcasted_iota(jnp.int32, (n_blk, LANES), 0)
    cur = (i * Q_BLOCK + lax.broadcasted_iota(jnp.int32, (n_blk, LANES), 1)) // SEL_BLOCK
    forced = (blk == 0) | (blk == cur) | (blk == cur - 1)
    score = jnp.where(blk > cur, -1.0, jnp.where(forced, FORCE_SCORE, imp))
    rank = jnp.zeros((n_blk, LANES), F32)
    for j in range(n_blk):
        other = jnp.broadcast_to(score[j:j + 1, :], (n_blk, LANES))
        beats = (other > score) | ((other == score) & (blk > j))
        rank = rank + jnp.where(beats, 1.0, 0.0)
    not_sel = jnp.where(rank < float(N_SEL), 0.0, 1.0)
    not_sel = jnp.concatenate([not_sel, jnp.zeros((LANES - n_blk, LANES), F32)], axis=0)
    not_sel_q = not_sel.T.astype(BF16)
    for c in range(HEAD_PAIRS):
        qs_ref[c * Q_BLOCK:(c + 1) * Q_BLOCK, LANES:] = not_sel_q

    M_E, M_O, A_E, A_O = (SUBLANES * r for r in range(4))
    st_row = lambda r: st_ref[r:r + 1, :]
    last_chunk = i // SEL_CHUNK

    def chunk_scores(r):
        for c in range(SEL_CHUNK):
            s_ref[c * rows2:(c + 1) * rows2, :] = _dot_nt(ks_ref[r * SEL_CHUNK + c], qs_ref[...])

    def chunk_pv(r):
        pv = None
        for c in range(SEL_CHUNK):
            d = _dot(vs_ref[r * SEL_CHUNK + c], p_ref[c * rows2:(c + 1) * rows2, :])
            pv = d if pv is None else pv + d
        return per_head_half(acc_ref[...], st_row(A_E), st_row(A_O)) + pv

    def chunk_softmax(causal_chunk):
        bias = None
        if causal_chunk is not None:
            bias = [tile_cols(jnp.where((causal_chunk * SEL_CHUNK + c) * KEY_TILE + row <= t_q, 0.0, MASK_VALUE))
                    for c in range(SEL_CHUNK)]
        p_new = [[None, None] for _ in range(SEL_CHUNK)]
        for hf, (m_r, a_r) in enumerate(((M_E, A_E), (M_O, A_O))):
            s = [s_ref[c * rows2 + hf * KEY_TILE:c * rows2 + (hf + 1) * KEY_TILE, :] for c in range(SEL_CHUNK)]
            if bias is not None:
                s = [s[c] + bias[c] for c in range(SEL_CHUNK)]
            m_prev = st_row(m_r)
            m_new = m_prev
            for c in range(SEL_CHUNK):
                m_new = jnp.maximum(m_new, jnp.max(s[c], axis=0, keepdims=True))
            for c in range(SEL_CHUNK):
                p_new[c][hf] = jnp.exp2(s[c] - m_new).astype(BF16)
            st_ref[m_r:m_r + 1, :] = m_new
            st_ref[a_r:a_r + 1, :] = jnp.exp2(m_prev - m_new)
        return [jnp.concatenate(p_new[c], axis=0) for c in range(SEL_CHUNK)]

    def store_p(p_new):
        for c in range(SEL_CHUNK):
            p_ref[c * rows2:(c + 1) * rows2, :] = p_new[c]

    chunk_scores(0)
    acc_ref[...] = jnp.zeros(acc_ref.shape, F32)
    p_ref[...] = jnp.zeros(p_ref.shape, BF16)
    st_ref[M_E:A_E, :] = jnp.full((2 * SUBLANES, cols_all), MASK_VALUE, F32)
    st_ref[A_E:, :] = jnp.zeros((2 * SUBLANES, cols_all), F32)

    def loop_step(r):
        acc_new = chunk_pv(jnp.maximum(r - 1, 0))
        p_new = chunk_softmax(None)
        acc_ref[...] = acc_new
        store_p(p_new)
        chunk_scores(r + 1)

    def finish():
        acc_new = chunk_pv(jnp.maximum(last_chunk - 1, 0))
        p_new = chunk_softmax(last_chunk)
        acc_ref[...] = acc_new
        store_p(p_new)
        o_sel = normalised(chunk_pv(last_chunk))
        for c in range(HEAD_PAIRS):
            cols = slice(c * Q_BLOCK, (c + 1) * Q_BLOCK)
            o_t = og_ref[:, cols] + gate_pair(1, c) * o_sel[:, cols]
            o_ref[q_rows, col0 + c * LANES:col0 + (c + 1) * LANES] = o_t.T

    return loop_step, finish


def _nsa_kernel(q_ref, gate_ref, qn_ref, kc_ref, vc_ref, ks_ref, vs_ref, kw_ref, vw_ref,
                o_ref, gt_ref, *group_scratch):
    step = pl.program_id(1)
    groups = []
    for qb in range(Q_PER_STEP):
        rows = slice(qb * Q_BLOCK, (qb + 1) * Q_BLOCK)
        gt_ref[qb] = jax.nn.sigmoid(gate_ref[rows, :]).T
        for g in range(KV_GROUPS):
            scratch = (ref.at[qb * KV_GROUPS + g] for ref in group_scratch)
            groups.append(_nsa_group(g, step * Q_PER_STEP + qb, qb * Q_BLOCK, q_ref, gt_ref.at[qb], qn_ref,
                                     kc_ref.at[g], vc_ref.at[g], ks_ref.at[g], vs_ref.at[g],
                                     kw_ref.at[g], vw_ref.at[g], o_ref, *scratch))

    def body(r, carry):
        for loop_step, _ in groups:
            loop_step(r)
        return carry

    lax.fori_loop(0, step, body, 0)
    for _, finish in groups:
        finish()


def _nsa(proj, qn, kc, vc, ks, vs, kw, vw, layer, B, S):
    step_rows = Q_BLOCK * Q_PER_STEP
    n_steps = S // step_rows
    gate_blk = COL_GATE // LANES
    per_batch = lambda a: pl.BlockSpec((None,) + a.shape[1:], lambda b, i: (b,) + (0,) * (a.ndim - 1))
    cols_all = HEAD_PAIRS * Q_BLOCK
    per_chain = lambda shape, dtype: pltpu.VMEM((Q_PER_STEP * KV_GROUPS,) + shape, dtype)
    return pl.pallas_call(
        _nsa_kernel,
        grid=(B, n_steps),
        in_specs=[
            pl.BlockSpec((step_rows, NSA_WIDTH), lambda b, i: (b * n_steps + i, 0)),
            pl.BlockSpec((step_rows, LANES), lambda b, i: (b * n_steps + i, gate_blk)),
            _layer_spec((1, LANES), layer, lambda b, i: (0, 0)),
            per_batch(kc), per_batch(vc), per_batch(ks), per_batch(vs), per_batch(kw), per_batch(vw),
        ],
        out_specs=pl.BlockSpec((step_rows, NSA_WIDTH), lambda b, i: (b * n_steps + i, 0)),
        out_shape=jax.ShapeDtypeStruct((B * S, NSA_WIDTH), F32),
        scratch_shapes=[
            pltpu.VMEM((Q_PER_STEP, LANES, Q_BLOCK), F32),
            per_chain((cols_all, 2 * LANES), BF16),
            per_chain((2 * N_CMP_PAD, cols_all), BF16),
            per_chain((IMP_PAD + N_CMP_PAD, LANES), F32),
            per_chain((LANES, cols_all), F32),
            per_chain((SEL_CHUNK * 2 * KEY_TILE, cols_all), F32),
            per_chain((SEL_CHUNK * 2 * KEY_TILE, cols_all), BF16),
            per_chain((V_ROWS, cols_all), F32),
            per_chain((4 * SUBLANES, cols_all), F32),
        ],
        compiler_params=_cparams(("parallel", "arbitrary")),
        name="nsa_attention",
    )(proj, proj, qn, kc, vc, ks, vs, kw, vw)


MIX_TM = 512
CARRY_ROWS = SUBLANES


def _mix_xattn_kernel(x_ref, a_ref, b_ref, c_ref, xv_ref, cw_ref, ga_ref, gb_ref, wa_ref, wb_ref,
                      g_ref, wq_ref, qn_ref, kn_ref, kv_ref, wo_ref, o_ref, carry_ref, oh_ref):
    @pl.when(pl.program_id(1) == 0)
    def _():
        carry_ref[...] = jnp.zeros(carry_ref.shape, F32)

    u = c_ref[...] * xv_ref[...]
    tm = u.shape[0]
    row = lax.broadcasted_iota(jnp.int32, u.shape, 0)
    prev1 = carry_ref[CARRY_ROWS - 1:CARRY_ROWS, :]
    prev2 = carry_ref[CARRY_ROWS - 2:CARRY_ROWS - 1, :]
    u1 = jnp.where(row == 0, prev1, pltpu.roll(u, 1, 0))
    u2 = jnp.where(row == 0, prev2, jnp.where(row == 1, prev1, pltpu.roll(u, 2, 0)))
    carry_ref[...] = u[tm - CARRY_ROWS:, :]
    o_b = b_ref[...] * (cw_ref[2:3, :] * u + cw_ref[1:2, :] * u1 + cw_ref[0:1, :] * u2)

    a = _rms(a_ref[...], ga_ref[...]).astype(BF16)
    b = _rms(o_b, gb_ref[...]).astype(BF16)
    x = x_ref[...] + _dot(a, wa_ref[...]) + _dot(b, wb_ref[...])

    h = _rms(x, g_ref[...]).astype(BF16)
    q = _dot(h, wq_ref[...])
    scale = XA_HEAD_DIM ** -0.5
    for hd in range(XA_HEADS):
        cols = slice(hd * XA_HEAD_DIM, (hd + 1) * XA_HEAD_DIM)
        qh = _rms(q[:, cols], qn_ref[...]).astype(BF16)
        kh = _rms(kv_ref[:, cols], kn_ref[...]).astype(BF16)
        vh = kv_ref[:, XA_WIDTH + hd * XA_HEAD_DIM:XA_WIDTH + (hd + 1) * XA_HEAD_DIM].astype(BF16)
        s = _dot_nt(qh, kh) * scale
        e = jnp.exp(s - jnp.max(s, axis=-1, keepdims=True))
        p = (e / jnp.sum(e, axis=-1, keepdims=True)).astype(BF16)
        oh_ref[:, cols] = _dot(p, vh).astype(BF16)
    o_ref[...] = x + _dot(oh_ref[...], wo_ref[...])


def _mix_xattn(x, o_a, proj, conv_w, ga, gb, w_out, g, wq, qn, kn, kv, wo, layer, B, S):
    M = kv.shape[0] // B
    n_t = S // MIX_TM
    conv_blk = COL_CONV // CONV_WIDTH
    tile = lambda width, col: pl.BlockSpec((MIX_TM, width), lambda b, i: (b * n_t + i, col))
    const = lambda shape, idx=(0, 0): _layer_spec(shape, layer, lambda b, i: idx, single_buffer=True)
    return pl.pallas_call(
        _mix_xattn_kernel,
        grid=(B, n_t),
        in_specs=[
            tile(D_MODEL, 0), tile(NSA_WIDTH, 0),
            tile(CONV_WIDTH, conv_blk), tile(CONV_WIDTH, conv_blk + 1), tile(CONV_WIDTH, conv_blk + 2),
            const((CONV_K, CONV_WIDTH)), const((1, NSA_WIDTH)), const((1, CONV_WIDTH)),
            const((NSA_WIDTH, D_MODEL)), const((CONV_WIDTH, D_MODEL), (1, 0)),
            const((1, D_MODEL)), const((D_MODEL, XA_WIDTH)),
            const((1, XA_HEAD_DIM)), const((1, XA_HEAD_DIM)),
            pl.BlockSpec((M, 2 * XA_WIDTH), lambda b, i: (b, 0)),
            const((XA_WIDTH, D_MODEL)),
        ],
        out_specs=tile(D_MODEL, 0),
        out_shape=jax.ShapeDtypeStruct((B * S, D_MODEL), F32),
        scratch_shapes=[pltpu.VMEM((CARRY_ROWS, CONV_WIDTH), F32), pltpu.VMEM((MIX_TM, XA_WIDTH), BF16)],
        compiler_params=_cparams(("parallel", "arbitrary")),
        name="mixer_out_xattn",
    )(x, o_a, proj, proj, proj, conv_w, ga, gb, w_out, w_out, g, wq, qn, kn, kv, wo)


def _dup(v):
    return jnp.concatenate([v, v], axis=-1)


def _selection_mask_tiles(S):
    j = jnp.arange(LANES)[None, None, :]
    k = (jnp.arange(S // KEY_TILE)[:, None, None] * KEY_TILE + jnp.arange(KEY_TILE)[None, :, None])
    return jnp.where(k // SEL_BLOCK == j, MASK_VALUE, 0.0).astype(BF16)


def kernel(x, mem, ffn1_norm, ffn1_w_gate, ffn1_w_up, ffn1_w_down, mix_norm, w_in, cmp_pe_k, cmp_w1_k, cmp_w2_k, cmp_pe_v, cmp_w1_v, cmp_w2_v, q_norm, k_norm, conv_w, out_norm_nsa, out_norm_conv, w_out, xattn_norm, mem_norm, xattn_w_q, xattn_w_kv, xattn_q_norm, xattn_k_norm, xattn_w_o, ffn2_norm, ffn2_w_gate, ffn2_w_up, ffn2_w_down):
    B, S, D = x.shape
    L = w_in.shape[0]
    T = B * S
    M = mem.shape[1]
    bf = lambda w: w.astype(BF16)
    row = lambda a: a.reshape(L, 1, a.shape[-1])

    up_rows, down_rows = D_MODEL // 8, D_FF // 8
    ffn_w = (_cast_layer(ffn1_w_gate, 0, up_rows), _cast_layer(ffn1_w_up, 0, up_rows),
             _cast_layer(ffn1_w_down, 0, down_rows))
    w_in_b = _w_in_layout(jnp.swapaxes(w_in, 1, 2))
    w_out_b = bf(w_out)
    wq_b, wkv_b, wo_b = bf(xattn_w_q), bf(xattn_w_kv), bf(xattn_w_o)
    w1k = bf(_dup(cmp_w1_k.reshape(L, CMP_BLOCK, HEAD_DIM, CMP_HIDDEN).swapaxes(2, 3)).swapaxes(2, 3))
    w1v = bf(_dup(cmp_w1_v.reshape(L, CMP_BLOCK, HEAD_DIM, CMP_HIDDEN).swapaxes(2, 3)).swapaxes(2, 3))
    w2k, w2v = bf(_dup(cmp_w2_k)), bf(_dup(cmp_w2_v))
    pek, pev = _dup(cmp_pe_k), _dup(cmp_pe_v)
    qn, kn = row(_dup(q_norm)), _dup(k_norm)
    eneg = _selection_mask_tiles(S)

    f1n, f2n, mixn = row(ffn1_norm), row(ffn2_norm), row(mix_norm)
    ona, onc = row(out_norm_nsa), row(out_norm_conv)
    xan, memn = row(xattn_norm), row(mem_norm)
    xqn, xkn = row(xattn_q_norm), row(xattn_k_norm)

    xs = x.reshape(T, D)
    mem2 = mem.reshape(B * M, D)
    for l in range(L):
        xs, ffn_w = _ffn(xs, f1n, l, *ffn_w, next_weights=(ffn2_w_gate, ffn2_w_up, ffn2_w_down, l))
        proj = _norm_matmul(xs, mixn, w_in_b, l, 1024, PROJ_TN, "mixer_in", w_transposed=True)
        kc, vc, ks, vs, kw, vw = _prep(proj, pek, w1k, w2k, pev, w1v, w2v, kn, eneg, l, B, S)
        o_a = _nsa(proj, qn, kc, vc, ks, vs, kw, vw, l, B, S)
        kv = _norm_matmul(mem2, memn, wkv_b, l, 512, 1024, "mem_kv")
        xs = _mix_xattn(xs, o_a, proj, conv_w, ona, onc, w_out_b, xan, wq_b, xqn, xkn, kv, wo_b, l, B, S)
        following = (ffn1_w_gate, ffn1_w_up, ffn1_w_down, l + 1) if l + 1 < L else None
        xs, ffn_w = _ffn(xs, f2n, l, *ffn_w, next_weights=following)
    return xs.reshape(B, S, D)
```
